```python
import math
import jax, jax.numpy as jnp
from jax import lax
import numpy as np


D_MODEL = 2048
BATCH = 16
SEQ = 2048
DEPTH = 4
DEC_BATCH = 32
DEC_SEQ = 32
PAST_LEN = 4096

CHUNK = 64
D_MIX = D_MODEL
HEAD_DIM = 64
D_ATTN = D_MIX // 2
D_GMLP = D_MIX - D_ATTN
N_HEADS = D_ATTN // HEAD_DIM
N_KV_HEADS = 4
GQA_GROUP = N_HEADS // N_KV_HEADS
D_KV = N_KV_HEADS * HEAD_DIM
WINDOW = 128
N_WIN_CHUNKS = WINDOW // CHUNK
N_BUCKETS = 32
MAX_DISTANCE = 128
GMLP_CHUNK = 128
N_GROUPS_B = D_GMLP // HEAD_DIM
D_PROJ = D_ATTN + 2 * D_KV + D_ATTN + 3 * D_GMLP
EPS = 1e-6
NEG_INF = -1e30

kernel_name = 'hymba_swa_sink_gmlp_stream_step'


def _rmsnorm(x, g):
    xf = x.astype(jnp.float32)
    y = xf * lax.rsqrt(jnp.mean(xf * xf, axis=-1, keepdims=True) + EPS)
    return (y * g.astype(jnp.float32)).astype(x.dtype)


def _layernorm(x, g, b):
    xf = x.astype(jnp.float32)
    mu = jnp.mean(xf, axis=-1, keepdims=True)
    xc = xf - mu
    y = xc * lax.rsqrt(jnp.mean(xc * xc, axis=-1, keepdims=True) + EPS)
    return (y * g.astype(jnp.float32) + b.astype(jnp.float32)).astype(x.dtype)


def _rel_bucket(rel):
    nb = N_BUCKETS // 2
    max_exact = nb // 2
    base = jnp.where(rel > 0, nb, 0)
    n = jnp.abs(rel)
    nf = jnp.maximum(n, 1).astype(jnp.float32)
    large = max_exact + (jnp.log(nf / max_exact) / math.log(MAX_DISTANCE / max_exact)
                         * (nb - max_exact)).astype(jnp.int32)
    large = jnp.minimum(large, nb - 1)
    return base + jnp.where(n < max_exact, n, large)


def _band_bias(rel_bias, n_q, n_past):
    i = jnp.arange(n_q)[:, None]
    j = jnp.arange(n_past + n_q)[None, :]
    b = rel_bias[_rel_bucket(j - n_past - i)]
    b = jnp.transpose(b, (2, 0, 1)).astype(jnp.float32)
    return b.reshape(N_KV_HEADS, GQA_GROUP, n_q, n_past + n_q)


def _band_attention(q, k, v, bias, valid, sinks):
    s = jnp.einsum('bnqhgd,bnkhd->bnhgqk', q, k).astype(jnp.float32) * (HEAD_DIM ** -0.5) + bias
    s = jnp.where(valid[None, :, None, None, None, :], s, NEG_INF)
    sink = sinks.astype(jnp.float32).reshape(N_KV_HEADS, GQA_GROUP)[None, None, :, :, None, None]
    m = jnp.maximum(jnp.max(s, axis=-1, keepdims=True), sink)
    p = jnp.exp(s - m)
    p = p / (jnp.sum(p, axis=-1, keepdims=True) + jnp.exp(sink - m))
    return jnp.einsum('bnhgqk,bnkhd->bnqhgd', p.astype(v.dtype), v)


def _project(x, g_in, w_in, ln_g, ln_b):
    B, S = x.shape[0], x.shape[1]
    h = _rmsnorm(x, g_in)
    p = jnp.einsum('bsd,de->bse', h, w_in)
    o1 = D_ATTN
    o2 = o1 + D_KV
    o3 = o2 + D_KV
    o4 = o3 + D_ATTN
    o5 = o4 + D_GMLP
    o6 = o5 + D_GMLP
    q = p[..., :o1].reshape(B, S, N_KV_HEADS, GQA_GROUP, HEAD_DIM)
    k = p[..., o1:o2].reshape(B, S, N_KV_HEADS, HEAD_DIM)
    v = p[..., o2:o3].reshape(B, S, N_KV_HEADS, HEAD_DIM)
    gate_a = p[..., o3:o4]
    u = jax.nn.gelu(p[..., o4:o5], approximate=False).reshape(B, S, N_GROUPS_B, HEAD_DIM)
    vg = _layernorm(jax.nn.gelu(p[..., o5:o6], approximate=False), ln_g, ln_b)
    vg = vg.reshape(B, S, N_GROUPS_B, HEAD_DIM)
    gate_b = p[..., o6:]
    return q, k, v, gate_a, u, vg, gate_b


def _attn_prompt(q, k, v, rel_bias, sinks):
    B, S = q.shape[0], q.shape[1]
    nC = S // CHUNK
    qc = q.reshape(B, nC, CHUNK, N_KV_HEADS, GQA_GROUP, HEAD_DIM)

    def band(t):
        tc = t.reshape(B, nC, CHUNK, N_KV_HEADS, HEAD_DIM)
        pad = jnp.zeros((B, N_WIN_CHUNKS, CHUNK, N_KV_HEADS, HEAD_DIM), t.dtype)
        tp = jnp.concatenate([pad, tc], axis=1)
        return jnp.concatenate([tp[:, w:w + nC] for w in range(N_WIN_CHUNKS + 1)], axis=2)

    kb = band(k)
    vb = band(v)
    key_pos = (jnp.arange(nC)[:, None] - N_WIN_CHUNKS) * CHUNK + jnp.arange(WINDOW + CHUNK)[None, :]
    valid = key_pos >= 0
    bias = _band_bias(rel_bias, CHUNK, WINDOW)
    o = _band_attention(qc, kb, vb, bias, valid, sinks)
    return o.reshape(B, S, D_ATTN), k[:, -WINDOW:], v[:, -WINDOW:]


def _attn_sample(q, k, v, ck, cv, rel_bias, sinks):
    B, T = q.shape[0], q.shape[1]
    n_past = ck.shape[1]
    kb = jnp.concatenate([ck, k.astype(ck.dtype)], axis=1)
    vb = jnp.concatenate([cv, v.astype(cv.dtype)], axis=1)
    bias = _band_bias(rel_bias, T, n_past)
    valid = jnp.ones((1, n_past + T), dtype=bool)
    o = _band_attention(q[:, None], kb[:, None], vb[:, None], bias, valid, sinks)
    return o.reshape(B, T, D_ATTN), kb[:, T:], vb[:, T:]


def _gmlp_mix(u, vg, w_s, b_s):
    L = vg.shape[2]
    ws = jnp.tril(w_s[:, :L, :L]).astype(vg.dtype)
    mix = jnp.einsum('gij,bcjgd->bcigd', ws, vg) + b_s[:, :L].T.astype(vg.dtype)[None, None, :, :, None]
    return u * mix


def _merge(attn_o, gmlp_o, gate_a, gate_b, g_attn, g_gmlp, w_out):
    ya = _rmsnorm(attn_o, g_attn) * jax.nn.silu(gate_a)
    yb = _rmsnorm(gmlp_o, g_gmlp) * jax.nn.silu(gate_b)
    return jnp.einsum('bse,ed->bsd', jnp.concatenate([ya, yb], axis=-1), w_out)


def setup_inputs(seed: int = 0) -> dict:
    key = jax.random.key(seed)
    ks = jax.random.split(key, 16)
    cache_rows = min(WINDOW, PAST_LEN)
    f32 = jnp.float32
    tril = jnp.tril(jnp.ones((GMLP_CHUNK, GMLP_CHUNK), f32))
    return {
        'x_prompt': jax.random.normal(ks[0], (BATCH, SEQ, D_MODEL), f32),
        'x_sample': jax.random.normal(ks[1], (DEC_BATCH, DEC_SEQ, D_MODEL), f32),
        'cache_k': jax.random.normal(ks[2], (DEPTH, DEC_BATCH, cache_rows, N_KV_HEADS, HEAD_DIM), f32),
        'cache_v': jax.random.normal(ks[3], (DEPTH, DEC_BATCH, cache_rows, N_KV_HEADS, HEAD_DIM), f32),
        'w_in': jax.random.normal(ks[4], (DEPTH, D_MODEL, D_PROJ), f32) * D_MODEL ** -0.5,
        'w_out': jax.random.normal(ks[5], (DEPTH, D_MIX, D_MODEL), f32) * D_MIX ** -0.5,
        'norm_in': 1.0 + 0.01 * jax.random.normal(ks[6], (DEPTH, D_MODEL), f32),
        'rel_bias': 0.1 * jax.random.normal(ks[7], (N_BUCKETS, N_HEADS), f32),
        'sinks': 0.5 * jax.random.normal(ks[8], (DEPTH, N_HEADS), f32),
        'norm_attn': 1.0 + 0.01 * jax.random.normal(ks[9], (DEPTH, D_ATTN), f32),
        'norm_gmlp': 1.0 + 0.01 * jax.random.normal(ks[10], (DEPTH, D_GMLP), f32),
        'ln_v_g': 1.0 + 0.01 * jax.random.normal(ks[11], (DEPTH, D_GMLP), f32),
        'ln_v_b': 0.01 * jax.random.normal(ks[12], (DEPTH, D_GMLP), f32),
        'w_spatial': jax.random.normal(ks[13], (DEPTH, N_GROUPS_B, GMLP_CHUNK, GMLP_CHUNK), f32)
                     * tril * GMLP_CHUNK ** -0.5,
        'b_spatial': 1.0 + 0.01 * jax.random.normal(ks[14], (DEPTH, N_GROUPS_B, GMLP_CHUNK), f32),
        'norm_final': 1.0 + 0.01 * jax.random.normal(ks[15], (D_MODEL,), f32),
    }


def reference(x_prompt, x_sample, cache_k, cache_v, w_in, w_out, norm_in, rel_bias, sinks,
              norm_attn, norm_gmlp, ln_v_g, ln_v_b, w_spatial, b_spatial, norm_final):
    xp = x_prompt
    xs = x_sample
    Bp, S = xp.shape[0], xp.shape[1]
    Bs, T = xs.shape[0], xs.shape[1]
    n_gchunks = S // GMLP_CHUNK
    kp_rows, vp_rows, ks_rows, vs_rows, vg_rows = [], [], [], [], []
    for l in range(DEPTH):
        q, k, v, ga, u, vg, gb = _project(xp, norm_in[l], w_in[l], ln_v_g[l], ln_v_b[l])
        ao, kw, vw = _attn_prompt(q, k, v, rel_bias, sinks[l])
        go = _gmlp_mix(u.reshape(Bp, n_gchunks, GMLP_CHUNK, N_GROUPS_B, HEAD_DIM),
                       vg.reshape(Bp, n_gchunks, GMLP_CHUNK, N_GROUPS_B, HEAD_DIM),
                       w_spatial[l], b_spatial[l]).reshape(Bp, S, D_GMLP)
        xp = xp + _merge(ao, go, ga, gb, norm_attn[l], norm_gmlp[l], w_out[l])
        kp_rows.append(kw)
        vp_rows.append(vw)
        q, k, v, ga, u, vg, gb = _project(xs, norm_in[l], w_in[l], ln_v_g[l], ln_v_b[l])
        ao, kw, vw = _attn_sample(q, k, v, cache_k[l], cache_v[l], rel_bias, sinks[l])
        go = _gmlp_mix(u[:, None], vg[:, None], w_spatial[l], b_spatial[l]).reshape(Bs, T, D_GMLP)
        xs = xs + _merge(ao, go, ga, gb, norm_attn[l], norm_gmlp[l], w_out[l])
        ks_rows.append(kw)
        vs_rows.append(vw)
        vg_rows.append(vg)
    y_prompt = _rmsnorm(xp, norm_final)
    y_sample = _rmsnorm(xs, norm_final)
    new_k_prompt = jnp.stack(kp_rows)
    new_v_prompt = jnp.stack(vp_rows)
    new_k_sample = jnp.stack(ks_rows)
    new_v_sample = jnp.stack(vs_rows)
    new_vgmlp_sample = jnp.stack(vg_rows)
    return (y_prompt, y_sample, new_k_prompt, new_v_prompt, new_k_sample, new_v_sample, new_vgmlp_sample)
```

```python
import functools
import math

import jax
import jax.numpy as jnp
from jax import lax
from jax.experimental import pallas as pl
from jax.experimental.pallas import tpu as pltpu

D_MODEL = 2048
DEPTH = 4
CHUNK = 64
HEAD_DIM = 64
D_ATTN = 1024
D_GMLP = 1024
N_HEADS = 16
N_KV_HEADS = 4
GQA_GROUP = 4
D_KV = 256
WINDOW = 128
N_BUCKETS = 32
MAX_DISTANCE = 128
GMLP_CHUNK = 128
N_GROUPS_B = 16
D_PROJ = 5632
EPS = 1e-6
NEG_INF = -1e30

O_Q, O_K, O_V, O_GA, O_U, O_VG, O_GB = 0, 1024, 1280, 1536, 2560, 3584, 4608

LANES = 128
N_PAIRS = N_GROUPS_B // 2
PAIR_ROWS = 2 * CHUNK
PAIR_KEYS = 4 * CHUNK
TM = 256
NB = 4
PROJ_COLS = 512
VMEM_LIMIT = 58 * 1024 * 1024

F32 = jnp.float32
BF16 = jnp.bfloat16


def _rel_bucket(rel):
    nb = N_BUCKETS // 2
    max_exact = nb // 2
    base = jnp.where(rel > 0, nb, 0)
    n = jnp.abs(rel)
    nf = jnp.maximum(n, 1).astype(F32)
    large = max_exact + (jnp.log(nf / max_exact) / math.log(MAX_DISTANCE / max_exact)
                         * (nb - max_exact)).astype(jnp.int32)
    large = jnp.minimum(large, nb - 1)
    return base + jnp.where(n < max_exact, n, large)


def _prompt_bias_tables(rel_bias):
    r = jnp.arange(PAIR_ROWS)[:, None]
    j = jnp.arange(PAIR_KEYS)[None, :]
    rel = (j - WINDOW) - r
    b = rel_bias[_rel_bucket(rel)].astype(F32)
    qc = r // CHUNK
    kc = j // CHUNK - 2
    in_window = (kc <= qc) & (kc >= qc - 2)
    tabs = []
    for first in (False, True):
        ok = in_window & (j >= WINDOW) if first else in_window
        t = jnp.where(ok[:, :, None], b, NEG_INF)
        t = jnp.transpose(t, (2, 0, 1)).reshape(N_KV_HEADS, GQA_GROUP * PAIR_ROWS, PAIR_KEYS)
        tabs.append(t)
    return jnp.stack(tabs)


def _sample_bias_table(rel_bias, n_q, n_past):
    i = jnp.arange(n_q)[:, None]
    j = jnp.arange(n_past + n_q)[None, :]
    b = rel_bias[_rel_bucket(j - n_past - i)].astype(F32)
    return jnp.transpose(b, (2, 0, 1)).reshape(N_KV_HEADS, GQA_GROUP * n_q, n_past + n_q)


def _pair_spatial(w_s, b_s, rows):
    w = w_s[:, :rows, :rows].reshape(N_PAIRS, 2, rows, rows)
    w = jnp.transpose(w, (0, 2, 1, 3)).reshape(N_PAIRS, rows, 2 * rows)
    b = b_s[:, :rows].reshape(N_PAIRS, 2, rows)
    b = jnp.repeat(jnp.transpose(b, (0, 2, 1)), LANES // 2, axis=2)
    return w, b


def _silu(x):
    return x / (1.0 + jnp.exp(-x))


def _gelu(x):
    return 0.5 * x * (1.0 + lax.erf(x * math.sqrt(0.5)))


def _project(x_ref, w_in_ref, g_in_ref, ln_g_ref, ln_b_ref,
             h_s, q_s, ga_s, gb_s, u_s, vgf_s, vlo_s, vhi_s):
    rows = x_ref.shape[0]
    x = x_ref[...]
    inv = lax.rsqrt(jnp.mean(x * x, axis=-1, keepdims=True) + EPS)
    h_s[...] = (x * inv * g_in_ref[...]).astype(BF16)

    def proj(off, width):
        return jnp.dot(h_s[...], w_in_ref[:, off:off + width], preferred_element_type=F32)

    for c in range(0, D_ATTN, PROJ_COLS):
        q_s[:, c:c + PROJ_COLS] = (proj(O_Q + c, PROJ_COLS) * (HEAD_DIM ** -0.5)).astype(BF16)
    kv = proj(O_K, 2 * D_KV)
    k = kv[:, :D_KV]
    v = kv[:, D_KV:]
    for c in range(0, D_ATTN, PROJ_COLS):
        ga_s[:, c:c + PROJ_COLS] = _silu(proj(O_GA + c, PROJ_COLS))
    for c in range(0, D_GMLP, PROJ_COLS):
        u_s[:, c:c + PROJ_COLS] = _gelu(proj(O_U + c, PROJ_COLS))
    for c in range(0, D_GMLP, PROJ_COLS):
        vgf_s[:, c:c + PROJ_COLS] = _gelu(proj(O_VG + c, PROJ_COLS))
    for c in range(0, D_GMLP, PROJ_COLS):
        gb_s[:, c:c + PROJ_COLS] = _silu(proj(O_GB + c, PROJ_COLS))

    g = vgf_s[...]
    mu = jnp.mean(g, axis=-1, keepdims=True)
    gc = g - mu
    y = gc * lax.rsqrt(jnp.mean(gc * gc, axis=-1, keepdims=True) + EPS)
    y = y * ln_g_ref[...] + ln_b_ref[...]
    vgf_s[...] = y
    low = (lax.broadcasted_iota(jnp.int32, (rows, D_GMLP), 1) % LANES) < (LANES // 2)
    vlo_s[...] = jnp.where(low, y, 0.0).astype(BF16)
    vhi_s[...] = jnp.where(low, 0.0, y).astype(BF16)
    return k, v


def _attend(q_rows, keys, vals, bias, sink_col):
    s = lax.dot_general(q_rows, keys, (((1,), (1,)), ((), ())), preferred_element_type=F32) + bias
    m = jnp.maximum(jnp.max(s, axis=-1, keepdims=True), sink_col)
    e = jnp.exp(s - m)
    l = jnp.sum(e, axis=-1, keepdims=True) + jnp.exp(sink_col - m)
    o = jnp.dot(e.astype(BF16), vals, preferred_element_type=F32)
    return o * (1.0 / l)


def _sink_col(sinks_ref, kvh, rows_per_head):
    return jnp.concatenate(
        [jnp.full((rows_per_head, 1), sinks_ref[kvh * GQA_GROUP + g], F32) for g in range(GQA_GROUP)],
        axis=0)


def _masked_spatial(wsp_ref, j, rows):
    w = wsp_ref[j]
    i = lax.broadcasted_iota(jnp.int32, w.shape, 0)
    c = lax.broadcasted_iota(jnp.int32, w.shape, 1) % rows
    return jnp.where(c <= i, w, 0.0).astype(BF16)


def _merge_and_out(x_ref, w_out_ref, g_attn_ref, g_gmlp_ref, g_fin_ref, o_ref,
                   ao_s, go_s, ga_s, gb_s, y_s, final):
    a = ao_s[...]
    ya = a * lax.rsqrt(jnp.mean(a * a, axis=-1, keepdims=True) + EPS) * g_attn_ref[...]
    y_s[:, :D_ATTN] = (ya * ga_s[...]).astype(BF16)
    b = go_s[...]
    yb = b * lax.rsqrt(jnp.mean(b * b, axis=-1, keepdims=True) + EPS) * g_gmlp_ref[...]
    y_s[:, D_ATTN:] = (yb * gb_s[...]).astype(BF16)
    ss = None
    for c in range(0, D_MODEL, PROJ_COLS):
        xn = x_ref[:, c:c + PROJ_COLS] + jnp.dot(
            y_s[...], w_out_ref[:, c:c + PROJ_COLS], preferred_element_type=F32)
        o_ref[:, c:c + PROJ_COLS] = xn
        if final:
            part = jnp.sum(xn * xn, axis=-1, keepdims=True)
            ss = part if ss is None else ss + part
    if final:
        o_ref[...] = o_ref[...] * lax.rsqrt(ss * (1.0 / D_MODEL) + EPS) * g_fin_ref[...]


def _prompt_kernel(sinks_ref, x_ref, w_in_ref, w_out_ref, wsp_ref, bsp_ref, bias_ref,
                   g_in_ref, g_attn_ref, g_gmlp_ref, ln_g_ref, ln_b_ref, g_fin_ref,
                   o_ref, k_out_ref, v_out_ref,
                   h_s, q_s, k_ext, v_ext, ga_s, gb_s, u_s, vgf_s, vlo_s, vhi_s, ao_s, y_s,
                   *, final):
    t = pl.program_id(1)

    @pl.when(t == 0)
    def _():
        k_ext[0:WINDOW, :] = jnp.zeros((WINDOW, D_KV), BF16)
        v_ext[0:WINDOW, :] = jnp.zeros((WINDOW, D_KV), BF16)

    k, v = _project(x_ref, w_in_ref, g_in_ref, ln_g_ref, ln_b_ref,
                    h_s, q_s, ga_s, gb_s, u_s, vgf_s, vlo_s, vhi_s)
    k_ext[WINDOW:, :] = k.astype(BF16)
    v_ext[WINDOW:, :] = v.astype(BF16)
    k_out_ref[...] = k[TM - WINDOW:, :]
    v_out_ref[...] = v[TM - WINDOW:, :]

    first = jnp.where(t == 0, 1, 0)
    for p in range(TM // PAIR_ROWS):
        r0 = p * PAIR_ROWS
        for kvh in range(N_KV_HEADS):
            c0 = kvh * GQA_GROUP * HEAD_DIM
            q_rows = jnp.concatenate(
                [q_s[r0:r0 + PAIR_ROWS, c0 + g * HEAD_DIM:c0 + (g + 1) * HEAD_DIM]
                 for g in range(GQA_GROUP)], axis=0)
            keys = k_ext[r0:r0 + PAIR_KEYS, kvh * HEAD_DIM:(kvh + 1) * HEAD_DIM]
            vals = v_ext[r0:r0 + PAIR_KEYS, kvh * HEAD_DIM:(kvh + 1) * HEAD_DIM]
            bias = bias_ref[first, kvh] if p == 0 else bias_ref[0, kvh]
            o = _attend(q_rows, keys, vals, bias, _sink_col(sinks_ref, kvh, PAIR_ROWS))
            for g in range(GQA_GROUP):
                ao_s[r0:r0 + PAIR_ROWS, c0 + g * HEAD_DIM:c0 + (g + 1) * HEAD_DIM] = (
                    o[g * PAIR_ROWS:(g + 1) * PAIR_ROWS, :])

    k_ext[0:WINDOW, :] = k_ext[TM:TM + WINDOW, :]
    v_ext[0:WINDOW, :] = v_ext[TM:TM + WINDOW, :]

    for j in range(N_PAIRS):
        w = _masked_spatial(wsp_ref, j, GMLP_CHUNK)
        cols = slice(j * LANES, (j + 1) * LANES)
        for c in range(TM // GMLP_CHUNK):
            rws = slice(c * GMLP_CHUNK, (c + 1) * GMLP_CHUNK)
            rhs = jnp.concatenate([vlo_s[rws, cols], vhi_s[rws, cols]], axis=0)
            mix = jnp.dot(w, rhs, preferred_element_type=F32) + bsp_ref[j]
            u_s[rws, cols] = u_s[rws, cols] * mix

    _merge_and_out(x_ref, w_out_ref, g_attn_ref, g_gmlp_ref, g_fin_ref, o_ref,
                   ao_s, u_s, ga_s, gb_s, y_s, final)


def _sample_kernel(sinks_ref, x_ref, ck_ref, cv_ref, w_in_ref, w_out_ref, wsp_ref, bsp_ref, bias_ref,
                   g_in_ref, g_attn_ref, g_gmlp_ref, ln_g_ref, ln_b_ref, g_fin_ref,
                   o_ref, nk_ref, nv_ref, vg_ref,
                   h_s, q_s, ga_s, gb_s, u_s, vlo_s, vhi_s, ao_s, y_s,
                   *, final, t_len):
    n_past = ck_ref.shape[1]
    k, v = _project(x_ref, w_in_ref, g_in_ref, ln_g_ref, ln_b_ref,
                    h_s, q_s, ga_s, gb_s, u_s, vg_ref, vlo_s, vhi_s)

    nk_ref[:, 0:n_past - t_len, :] = ck_ref[:, t_len:, :]
    nv_ref[:, 0:n_past - t_len, :] = cv_ref[:, t_len:, :]
    nk_ref[:, n_past - t_len:, :] = k.reshape(NB, t_len, D_KV)
    nv_ref[:, n_past - t_len:, :] = v.reshape(NB, t_len, D_KV)

    kb = k.astype(BF16)
    vb = v.astype(BF16)
    for b in range(NB):
        r0 = b * t_len
        for kvh in range(N_KV_HEADS):
            c0 = kvh * GQA_GROUP * HEAD_DIM
            hc = slice(kvh * HEAD_DIM, (kvh + 1) * HEAD_DIM)
            q_rows = jnp.concatenate(
                [q_s[r0:r0 + t_len, c0 + g * HEAD_DIM:c0 + (g + 1) * HEAD_DIM]
                 for g in range(GQA_GROUP)], axis=0)
            keys = jnp.concatenate([ck_ref[b, :, hc].astype(BF16), kb[r0:r0 + t_len, hc]], axis=0)
            vals = jnp.concatenate([cv_ref[b, :, hc].astype(BF16), vb[r0:r0 + t_len, hc]], axis=0)
            o = _attend(q_rows, keys, vals, bias_ref[kvh], _sink_col(sinks_ref, kvh, t_len))
            for g in range(GQA_GROUP):
                ao_s[r0:r0 + t_len, c0 + g * HEAD_DIM:c0 + (g + 1) * HEAD_DIM] = (
                    o[g * t_len:(g + 1) * t_len, :])

    for j in range(N_PAIRS):
        w = _masked_spatial(wsp_ref, j, t_len)
        cols = slice(j * LANES, (j + 1) * LANES)
        rhs = jnp.concatenate(
            [jnp.concatenate([vlo_s[b * t_len:(b + 1) * t_len, cols],
                              vhi_s[b * t_len:(b + 1) * t_len, cols]], axis=0)
             for b in range(NB)], axis=1)
        mix = jnp.dot(w, rhs, preferred_element_type=F32)
        for b in range(NB):
            rws = slice(b * t_len, (b + 1) * t_len)
            u_s[rws, cols] = u_s[rws, cols] * (mix[:, b * LANES:(b + 1) * LANES] + bsp_ref[j])

    _merge_and_out(x_ref, w_out_ref, g_attn_ref, g_gmlp_ref, g_fin_ref, o_ref,
                   ao_s, u_s, ga_s, gb_s, y_s, final)


def _resident(shape):
    nd = len(shape)
    return pl.BlockSpec(shape, lambda *_: (0,) * nd, pipeline_mode=pl.Buffered(1))


def _weight_specs(wsp, bsp, bias):
    return [
        _resident((D_MODEL, D_PROJ)), _resident((D_MODEL, D_MODEL)),
        _resident(wsp.shape), _resident(bsp.shape), _resident(bias.shape),
        _resident((1, D_MODEL)), _resident((1, D_ATTN)), _resident((1, D_GMLP)),
        _resident((1, D_GMLP)), _resident((1, D_GMLP)), _resident((1, D_MODEL)),
    ]


def _prompt_layer(x, sinks, w_in, w_out, wsp, bsp, bias, vecs, final):
    batch, seq, _ = x.shape
    kv_shape = jax.ShapeDtypeStruct((batch, WINDOW, D_KV), F32)
    kv_spec = pl.BlockSpec((None, WINDOW, D_KV), lambda b, t: (b, 0, 0))
    x_spec = pl.BlockSpec((None, TM, D_MODEL), lambda b, t: (b, t, 0))
    return pl.pallas_call(
        functools.partial(_prompt_kernel, final=final),
        grid=(batch, seq // TM),
        in_specs=[pl.BlockSpec(memory_space=pltpu.SMEM), x_spec] + _weight_specs(wsp, bsp, bias),
        out_specs=[x_spec, kv_spec, kv_spec],
        out_shape=[jax.ShapeDtypeStruct(x.shape, F32), kv_shape, kv_shape],
        scratch_shapes=[
            pltpu.VMEM((TM, D_MODEL), BF16),
            pltpu.VMEM((TM, D_ATTN), BF16),
            pltpu.VMEM((WINDOW + TM, D_KV), BF16),
            pltpu.VMEM((WINDOW + TM, D_KV), BF16),
            pltpu.VMEM((TM, D_ATTN), F32),
            pltpu.VMEM((TM, D_GMLP), F32),
            pltpu.VMEM((TM, D_GMLP), F32),
            pltpu.VMEM((TM, D_GMLP), F32),
            pltpu.VMEM((TM, D_GMLP), BF16),
            pltpu.VMEM((TM, D_GMLP), BF16),
            pltpu.VMEM((TM, D_ATTN), F32),
            pltpu.VMEM((TM, D_MODEL), BF16),
        ],
        compiler_params=pltpu.CompilerParams(
            dimension_semantics=("arbitrary", "arbitrary"), vmem_limit_bytes=VMEM_LIMIT),
        name="prompt_layer_final" if final else "prompt_layer",
    )(sinks, x, w_in, w_out, wsp, bsp, bias, *vecs)


def _sample_layer(x, ck, cv, sinks, w_in, w_out, wsp, bsp, bias, vecs, final, t_len):
    rows = x.shape[0]
    n_streams, n_past, _ = ck.shape
    step_rows = NB * t_len
    x_spec = pl.BlockSpec((step_rows, D_MODEL), lambda i: (i, 0))
    c_spec = pl.BlockSpec((NB, n_past, D_KV), lambda i: (i, 0, 0))
    vg_spec = pl.BlockSpec((step_rows, D_GMLP), lambda i: (i, 0))
    return pl.pallas_call(
        functools.partial(_sample_kernel, final=final, t_len=t_len),
        grid=(n_streams // NB,),
        in_specs=[pl.BlockSpec(memory_space=pltpu.SMEM), x_spec, c_spec, c_spec]
        + _weight_specs(wsp, bsp, bias),
        out_specs=[x_spec, c_spec, c_spec, vg_spec],
        out_shape=[jax.ShapeDtypeStruct(x.shape, F32),
                   jax.ShapeDtypeStruct(ck.shape, F32), jax.ShapeDtypeStruct(cv.shape, F32),
                   jax.ShapeDtypeStruct((rows, D_GMLP), F32)],
        scratch_shapes=[
            pltpu.VMEM((step_rows, D_MODEL), BF16),
            pltpu.VMEM((step_rows, D_ATTN), BF16),
            pltpu.VMEM((step_rows, D_ATTN), F32),
            pltpu.VMEM((step_rows, D_GMLP), F32),
            pltpu.VMEM((step_rows, D_GMLP), F32),
            pltpu.VMEM((step_rows, D_GMLP), BF16),
            pltpu.VMEM((step_rows, D_GMLP), BF16),
            pltpu.VMEM((step_rows, D_ATTN), F32),
            pltpu.VMEM((step_rows, D_MODEL), BF16),
        ],
        compiler_params=pltpu.CompilerParams(
            dimension_semantics=("arbitrary",), vmem_limit_bytes=VMEM_LIMIT),
        name="sample_layer_final" if final else "sample_layer",
    )(sinks, x, ck, cv, w_in, w_out, wsp, bsp, bias, *vecs)


def kernel(x_prompt, x_sample, cache_k, cache_v, w_in, w_out, norm_in, rel_bias, sinks,
           norm_attn, norm_gmlp, ln_v_g, ln_v_b, w_spatial, b_spatial, norm_final):
    bp, seq, _ = x_prompt.shape
    bs, t_len, _ = x_sample.shape
    n_past = cache_k.shape[2]
    assert seq % TM == 0 and TM % PAIR_ROWS == 0 and TM % GMLP_CHUNK == 0 and TM >= WINDOW
    assert bs % NB == 0 and t_len <= GMLP_CHUNK and n_past >= t_len

    bias_p = _prompt_bias_tables(rel_bias)
    bias_s = _sample_bias_table(rel_bias, t_len, n_past)
    w_in_b = w_in.astype(BF16)
    w_out_b = w_out.astype(BF16)
    ck = cache_k.reshape(DEPTH, bs, n_past, D_KV)
    cv = cache_v.reshape(DEPTH, bs, n_past, D_KV)

    xp = x_prompt
    xs = x_sample.reshape(bs * t_len, D_MODEL)
    kp, vp, ks, vs, vgs = [], [], [], [], []
    for l in range(DEPTH):
        final = l == DEPTH - 1
        vecs = (norm_in[l][None], norm_attn[l][None], norm_gmlp[l][None],
                ln_v_g[l][None], ln_v_b[l][None], norm_final[None])
        wsp_p, bsp_p = _pair_spatial(w_spatial[l], b_spatial[l], GMLP_CHUNK)
        wsp_s, bsp_s = _pair_spatial(w_spatial[l], b_spatial[l], t_len)
        xp, k_l, v_l = _prompt_layer(xp, sinks[l], w_in_b[l], w_out_b[l], wsp_p, bsp_p, bias_p,
                                     vecs, final)
        kp.append(k_l)
        vp.append(v_l)
        xs, nk, nv, vg = _sample_layer(xs, ck[l], cv[l], sinks[l], w_in_b[l], w_out_b[l],
                                       wsp_s, bsp_s, bias_s, vecs, final, t_len)
        ks.append(nk)
        vs.append(nv)
        vgs.append(vg)

    def heads(rows, n):
        return jnp.stack(rows).reshape(DEPTH, n, -1, N_KV_HEADS, HEAD_DIM)

    return (xp, xs.reshape(bs, t_len, D_MODEL),
            heads(kp, bp), heads(vp, bp), heads(ks, bs), heads(vs, bs),
            jnp.stack(vgs).reshape(DEPTH, bs, t_len, N_GROUPS_B, HEAD_DIM))
```

```python
import functools
import math

import jax
import jax.numpy as jnp
from jax import lax
from jax.experimental import pallas as pl
from jax.experimental.pallas import tpu as pltpu

D_MODEL = 2048
DEPTH = 4
CHUNK = 64
HEAD_DIM = 64
D_ATTN = 1024
D_GMLP = 1024
N_HEADS = 16
N_KV_HEADS = 4
GQA_GROUP = 4
D_KV = 256
WINDOW = 128
N_BUCKETS = 32
MAX_DISTANCE = 128
GMLP_CHUNK = 128
N_GROUPS_B = 16
D_PROJ = 5632
EPS = 1e-6
NEG_INF = -1e30

O_Q, O_K, O_V, O_GA, O_U, O_VG, O_GB = 0, 1024, 1280, 1536, 2560, 3584, 4608

LANES = 128
N_PAIRS = N_GROUPS_B // 2
PAIR_ROWS = 2 * CHUNK
PAIR_KEYS = 4 * CHUNK
TM = 256
NB = 4
PROJ_COLS = 512
ROW_BLOCK = 32
VMEM_LIMIT = 58 * 1024 * 1024

F32 = jnp.float32
BF16 = jnp.bfloat16


def _rel_bucket(rel):
    nb = N_BUCKETS // 2
    max_exact = nb // 2
    base = jnp.where(rel > 0, nb, 0)
    n = jnp.abs(rel)
    nf = jnp.maximum(n, 1).astype(F32)
    large = max_exact + (jnp.log(nf / max_exact) / math.log(MAX_DISTANCE / max_exact)
                         * (nb - max_exact)).astype(jnp.int32)
    large = jnp.minimum(large, nb - 1)
    return base + jnp.where(n < max_exact, n, large)


def _prompt_bias_tables(rel_bias):
    r = jnp.arange(PAIR_ROWS)[None, :]
    j = jnp.arange(PAIR_KEYS)[:, None]
    rel = (j - WINDOW) - r
    b = rel_bias[_rel_bucket(rel)].astype(F32)
    qc = r // CHUNK
    kc = j // CHUNK - 2
    in_window = (kc <= qc) & (kc >= qc - 2)
    tabs = []
    for first in (False, True):
        ok = (in_window & (j >= WINDOW)) if first else in_window
        t = jnp.where(ok[:, :, None], b, NEG_INF)
        t = jnp.transpose(t, (2, 0, 1)).reshape(N_KV_HEADS, GQA_GROUP, PAIR_KEYS, PAIR_ROWS)
        t = jnp.transpose(t, (0, 2, 1, 3)).reshape(N_KV_HEADS, PAIR_KEYS, GQA_GROUP * PAIR_ROWS)
        tabs.append(t)
    return jnp.stack(tabs)


def _sample_bias_table(rel_bias, n_q, n_past):
    i = jnp.arange(n_q)[:, None]
    j = jnp.arange(n_past + n_q)[None, :]
    b = rel_bias[_rel_bucket(j - n_past - i)].astype(F32)
    return jnp.transpose(b, (2, 0, 1)).reshape(N_KV_HEADS, GQA_GROUP * n_q, n_past + n_q)


def _pair_spatial(w_s, b_s, rows):
    w = w_s[:, :rows, :rows].reshape(N_PAIRS, 2, rows, rows)
    w = jnp.transpose(w, (0, 2, 1, 3)).reshape(N_PAIRS, rows, 2 * rows)
    b = b_s[:, :rows].reshape(N_PAIRS, 2, rows)
    b = jnp.repeat(jnp.transpose(b, (0, 2, 1)), LANES // 2, axis=2)
    return w, b


def _heads_g_major(a, axis):
    shp = a.shape
    a = a.reshape(shp[:axis] + (N_KV_HEADS, GQA_GROUP, HEAD_DIM) + shp[axis + 1:])
    return jnp.swapaxes(a, axis, axis + 1).reshape(shp)


def _silu(x):
    return x / (1.0 + jnp.exp(-x))


def _gelu(x):
    return 0.5 * x * (1.0 + lax.erf(x * math.sqrt(0.5)))


def _project(x_ref, w_in_ref, g_in_ref, ln_g_ref, ln_b_ref,
             h_s, q_s, ga_s, gb_s, u_s, vgf_s, vlo_s, vhi_s):
    rows = x_ref.shape[0]
    for r in range(0, rows, ROW_BLOCK):
        x = x_ref[r:r + ROW_BLOCK, :]
        inv = lax.rsqrt(jnp.mean(x * x, axis=-1, keepdims=True) + EPS)
        h_s[r:r + ROW_BLOCK, :] = (x * inv * g_in_ref[...]).astype(BF16)

    def proj(off, width):
        return jnp.dot(h_s[...], w_in_ref[:, off:off + width], preferred_element_type=F32)

    for c in range(0, D_GMLP, PROJ_COLS):
        vgf_s[:, c:c + PROJ_COLS] = _gelu(proj(O_VG + c, PROJ_COLS))
    for c in range(0, D_GMLP, PROJ_COLS):
        u_s[:, c:c + PROJ_COLS] = _gelu(proj(O_U + c, PROJ_COLS))
    for c in range(0, D_GMLP, PROJ_COLS):
        gb_s[:, c:c + PROJ_COLS] = _silu(proj(O_GB + c, PROJ_COLS))
    kv = proj(O_K, 2 * D_KV)
    k = kv[:, :D_KV]
    v = kv[:, D_KV:]
    for c in range(0, D_ATTN, PROJ_COLS):
        q_s[:, c:c + PROJ_COLS] = (proj(O_Q + c, PROJ_COLS) * (HEAD_DIM ** -0.5)).astype(BF16)
    for c in range(0, D_ATTN, PROJ_COLS):
        ga_s[:, c:c + PROJ_COLS] = _silu(proj(O_GA + c, PROJ_COLS))

    low = (lax.broadcasted_iota(jnp.int32, (ROW_BLOCK, D_GMLP), 1) % LANES) < (LANES // 2)
    for r in range(0, rows, ROW_BLOCK):
        g = vgf_s[r:r + ROW_BLOCK, :]
        gc = g - jnp.mean(g, axis=-1, keepdims=True)
        y = gc * lax.rsqrt(jnp.mean(gc * gc, axis=-1, keepdims=True) + EPS)
        y = y * ln_g_ref[...] + ln_b_ref[...]
        vgf_s[r:r + ROW_BLOCK, :] = y
        vlo_s[r:r + ROW_BLOCK, :] = jnp.where(low, y, 0.0).astype(BF16)
        vhi_s[r:r + ROW_BLOCK, :] = jnp.where(low, 0.0, y).astype(BF16)
    return k, v


def _attend(q_rows, keys, vals, bias, sink_col):
    s = lax.dot_general(q_rows, keys, (((1,), (1,)), ((), ())), preferred_element_type=F32) + bias
    m = jnp.maximum(jnp.max(s, axis=-1, keepdims=True), sink_col)
    e = jnp.exp(s - m)
    l = jnp.sum(e, axis=-1, keepdims=True) + jnp.exp(sink_col - m)
    o = jnp.dot(e.astype(BF16), vals, preferred_element_type=F32)
    return o * (1.0 / l)


def _attend_t(keys, q_rows, vals_t, bias_t, sink_row):
    s = lax.dot_general(keys, q_rows, (((1,), (1,)), ((), ())), preferred_element_type=F32) + bias_t
    m = jnp.maximum(jnp.max(s, axis=0, keepdims=True), sink_row)
    e = jnp.exp(s - m)
    l = jnp.sum(e, axis=0, keepdims=True) + jnp.exp(sink_row - m)
    o = jnp.dot(vals_t, e.astype(BF16), preferred_element_type=F32)
    return o * (1.0 / l)


def _sink_vec(sinks_ref, kvh, n, axis):
    shape = (n, 1) if axis == 0 else (1, n)
    return jnp.concatenate(
        [jnp.full(shape, sinks_ref[kvh * GQA_GROUP + g], F32) for g in range(GQA_GROUP)], axis=axis)


def _masked_spatial(wsp_ref, j, rows):
    w = wsp_ref[j]
    i = lax.broadcasted_iota(jnp.int32, w.shape, 0)
    c = lax.broadcasted_iota(jnp.int32, w.shape, 1) % rows
    return jnp.where(c <= i, w, 0.0).astype(BF16)


def _gated_norm(src_s, gate_s, g_ref, y_s, col0):
    rows, width = src_s.shape
    for r in range(0, rows, ROW_BLOCK):
        a = src_s[r:r + ROW_BLOCK, :]
        inv = lax.rsqrt(jnp.mean(a * a, axis=-1, keepdims=True) + EPS)
        y_s[r:r + ROW_BLOCK, col0:col0 + width] = (
            a * inv * g_ref[...] * gate_s[r:r + ROW_BLOCK, :]).astype(BF16)


def _out_gmlp_half(x_ref, w_out_ref, o_ref, y_s):
    for c in range(0, D_MODEL, PROJ_COLS):
        o_ref[:, c:c + PROJ_COLS] = x_ref[:, c:c + PROJ_COLS] + jnp.dot(
            y_s[:, D_ATTN:], w_out_ref[D_ATTN:, c:c + PROJ_COLS], preferred_element_type=F32)


def _out_attn_half(w_out_ref, g_fin_ref, o_ref, y_s, final):
    rows = o_ref.shape[0]
    for c in range(0, D_MODEL, PROJ_COLS):
        o_ref[:, c:c + PROJ_COLS] = o_ref[:, c:c + PROJ_COLS] + jnp.dot(
            y_s[:, :D_ATTN], w_out_ref[:D_ATTN, c:c + PROJ_COLS], preferred_element_type=F32)
    if final:
        for r in range(0, rows, ROW_BLOCK):
            xn = o_ref[r:r + ROW_BLOCK, :]
            inv = lax.rsqrt(jnp.mean(xn * xn, axis=-1, keepdims=True) + EPS)
            o_ref[r:r + ROW_BLOCK, :] = xn * inv * g_fin_ref[...]


def _prompt_kernel(sinks_ref, x_ref, w_in_ref, w_out_ref, wsp_ref, bsp_ref, bias_ref,
                   g_in_ref, g_attn_ref, g_gmlp_ref, ln_g_ref, ln_b_ref, g_fin_ref,
                   o_ref, k_out_ref, v_out_ref,
                   h_s, q_s, k_ext, vt_ext, ga_s, gb_s, u_s, vlo_s, vhi_s, ao_s, y_s,
                   *, final):
    t = pl.program_id(1)

    @pl.when(t == 0)
    def _():
        k_ext[0:WINDOW, :] = jnp.zeros((WINDOW, D_KV), BF16)
        vt_ext[:, 0:WINDOW] = jnp.zeros((D_KV, WINDOW), BF16)

    k, v = _project(x_ref, w_in_ref, g_in_ref, ln_g_ref, ln_b_ref,
                    h_s, q_s, ga_s, gb_s, u_s, ao_s, vlo_s, vhi_s)
    k_ext[WINDOW:, :] = k.astype(BF16)
    vt_ext[:, WINDOW:] = v.T.astype(BF16)
    k_out_ref[...] = k[TM - WINDOW:, :]
    v_out_ref[...] = v[TM - WINDOW:, :]

    for j in range(N_PAIRS):
        w = _masked_spatial(wsp_ref, j, GMLP_CHUNK)
        cols = slice(j * LANES, (j + 1) * LANES)
        for c in range(TM // GMLP_CHUNK):
            rws = slice(c * GMLP_CHUNK, (c + 1) * GMLP_CHUNK)
            rhs = jnp.concatenate([vlo_s[rws, cols], vhi_s[rws, cols]], axis=0)
            mix = jnp.dot(w, rhs, preferred_element_type=F32) + bsp_ref[j]
            u_s[rws, cols] = u_s[rws, cols] * mix
    _gated_norm(u_s, gb_s, g_gmlp_ref, y_s, D_ATTN)
    _out_gmlp_half(x_ref, w_out_ref, o_ref, y_s)

    first = jnp.where(t == 0, 1, 0)
    for p in range(TM // PAIR_ROWS):
        r0 = p * PAIR_ROWS
        o_t = []
        for kvh in range(N_KV_HEADS):
            c0 = kvh * GQA_GROUP * HEAD_DIM
            hc = slice(kvh * HEAD_DIM, (kvh + 1) * HEAD_DIM)
            q_rows = jnp.concatenate(
                [q_s[r0:r0 + PAIR_ROWS, c0 + g * HEAD_DIM:c0 + (g + 1) * HEAD_DIM]
                 for g in range(GQA_GROUP)], axis=0)
            bias_t = bias_ref[first, kvh] if p == 0 else bias_ref[0, kvh]
            o_t.append(_attend_t(k_ext[r0:r0 + PAIR_KEYS, hc], q_rows,
                                 vt_ext[hc, r0:r0 + PAIR_KEYS], bias_t,
                                 _sink_vec(sinks_ref, kvh, PAIR_ROWS, 1)))
        o_t = jnp.concatenate(o_t, axis=0)
        for g in range(GQA_GROUP):
            ao_s[r0:r0 + PAIR_ROWS, g * D_KV:(g + 1) * D_KV] = (
                o_t[:, g * PAIR_ROWS:(g + 1) * PAIR_ROWS].T)

    k_ext[0:WINDOW, :] = k_ext[TM:TM + WINDOW, :]
    vt_ext[:, 0:WINDOW] = vt_ext[:, TM:TM + WINDOW]

    _gated_norm(ao_s, ga_s, g_attn_ref, y_s, 0)
    _out_attn_half(w_out_ref, g_fin_ref, o_ref, y_s, final)


def _sample_kernel(sinks_ref, x_ref, ck_ref, cv_ref, w_in_ref, w_out_ref, wsp_ref, bsp_ref, bias_ref,
                   g_in_ref, g_attn_ref, g_gmlp_ref, ln_g_ref, ln_b_ref, g_fin_ref,
                   o_ref, nk_ref, nv_ref, vg_ref,
                   h_s, q_s, ga_s, gb_s, u_s, vlo_s, vhi_s, ao_s, y_s,
                   *, final, t_len):
    n_past = ck_ref.shape[1]
    k, v = _project(x_ref, w_in_ref, g_in_ref, ln_g_ref, ln_b_ref,
                    h_s, q_s, ga_s, gb_s, u_s, vg_ref, vlo_s, vhi_s)

    nk_ref[:, 0:n_past - t_len, :] = ck_ref[:, t_len:, :]
    nv_ref[:, 0:n_past - t_len, :] = cv_ref[:, t_len:, :]
    nk_ref[:, n_past - t_len:, :] = k.reshape(NB, t_len, D_KV)
    nv_ref[:, n_past - t_len:, :] = v.reshape(NB, t_len, D_KV)

    for j in range(N_PAIRS):
        w = _masked_spatial(wsp_ref, j, t_len)
        cols = slice(j * LANES, (j + 1) * LANES)
        rhs = jnp.concatenate(
            [jnp.concatenate([vlo_s[b * t_len:(b + 1) * t_len, cols],
                              vhi_s[b * t_len:(b + 1) * t_len, cols]], axis=0)
             for b in range(NB)], axis=1)
        mix = jnp.dot(w, rhs, preferred_element_type=F32)
        for b in range(NB):
            rws = slice(b * t_len, (b + 1) * t_len)
            u_s[rws, cols] = u_s[rws, cols] * (mix[:, b * LANES:(b + 1) * LANES] + bsp_ref[j])
    _gated_norm(u_s, gb_s, g_gmlp_ref, y_s, D_ATTN)
    _out_gmlp_half(x_ref, w_out_ref, o_ref, y_s)

    kb = k.astype(BF16)
    vb = v.astype(BF16)
    for b in range(NB):
        r0 = b * t_len
        for kvh in range(N_KV_HEADS):
            c0 = kvh * GQA_GROUP * HEAD_DIM
            hc = slice(kvh * HEAD_DIM, (kvh + 1) * HEAD_DIM)
            q_rows = jnp.concatenate(
                [q_s[r0:r0 + t_len, c0 + g * HEAD_DIM:c0 + (g + 1) * HEAD_DIM]
                 for g in range(GQA_GROUP)], axis=0)
            keys = jnp.concatenate([ck_ref[b, :, hc].astype(BF16), kb[r0:r0 + t_len, hc]], axis=0)
            vals = jnp.concatenate([cv_ref[b, :, hc].astype(BF16), vb[r0:r0 + t_len, hc]], axis=0)
            o = _attend(q_rows, keys, vals, bias_ref[kvh], _sink_vec(sinks_ref, kvh, t_len, 0))
            for g in range(GQA_GROUP):
                cg = g * D_KV + kvh * HEAD_DIM
                ao_s[r0:r0 + t_len, cg:cg + HEAD_DIM] = o[g * t_len:(g + 1) * t_len, :]

    _gated_norm(ao_s, ga_s, g_attn_ref, y_s, 0)
    _out_attn_half(w_out_ref, g_fin_ref, o_ref, y_s, final)


def _resident(shape):
    nd = len(shape)
    return pl.BlockSpec(shape, lambda *_: (0,) * nd, pipeline_mode=pl.Buffered(1))


def _weight_specs(wsp, bsp, bias):
    return [
        _resident((D_MODEL, D_PROJ)), _resident((D_MODEL, D_MODEL)),
        _resident(wsp.shape), _resident(bsp.shape), _resident(bias.shape),
        _resident((1, D_MODEL)), _resident((1, D_ATTN)), _resident((1, D_GMLP)),
        _resident((1, D_GMLP)), _resident((1, D_GMLP)), _resident((1, D_MODEL)),
    ]


def _prompt_layer(x, sinks, w_in, w_out, wsp, bsp, bias, vecs, final):
    batch, seq, _ = x.shape
    kv_shape = jax.ShapeDtypeStruct((batch, WINDOW, D_KV), F32)
    kv_spec = pl.BlockSpec((None, WINDOW, D_KV), lambda b, t: (b, 0, 0))
    x_spec = pl.BlockSpec((None, TM, D_MODEL), lambda b, t: (b, t, 0))
    return pl.pallas_call(
        functools.partial(_prompt_kernel, final=final),
        grid=(batch, seq // TM),
        in_specs=[pl.BlockSpec(memory_space=pltpu.SMEM), x_spec] + _weight_specs(wsp, bsp, bias),
        out_specs=[x_spec, kv_spec, kv_spec],
        out_shape=[jax.ShapeDtypeStruct(x.shape, F32), kv_shape, kv_shape],
        scratch_shapes=[
            pltpu.VMEM((TM, D_MODEL), BF16),
            pltpu.VMEM((TM, D_ATTN), BF16),
            pltpu.VMEM((WINDOW + TM, D_KV), BF16),
            pltpu.VMEM((D_KV, WINDOW + TM), BF16),
            pltpu.VMEM((TM, D_ATTN), F32),
            pltpu.VMEM((TM, D_GMLP), F32),
            pltpu.VMEM((TM, D_GMLP), F32),
            pltpu.VMEM((TM, D_GMLP), BF16),
            pltpu.VMEM((TM, D_GMLP), BF16),
            pltpu.VMEM((TM, D_ATTN), F32),
            pltpu.VMEM((TM, D_MODEL), BF16),
        ],
        compiler_params=pltpu.CompilerParams(
            dimension_semantics=("arbitrary", "arbitrary"), vmem_limit_bytes=VMEM_LIMIT),
        name="prompt_layer_final" if final else "prompt_layer",
    )(sinks, x, w_in, w_out, wsp, bsp, bias, *vecs)


def _sample_layer(x, ck, cv, sinks, w_in, w_out, wsp, bsp, bias, vecs, final, t_len):
    rows = x.shape[0]
    n_streams, n_past, _ = ck.shape
    step_rows = NB * t_len
    x_spec = pl.BlockSpec((step_rows, D_MODEL), lambda i: (i, 0))
    c_spec = pl.BlockSpec((NB, n_past, D_KV), lambda i: (i, 0, 0))
    vg_spec = pl.BlockSpec((step_rows, D_GMLP), lambda i: (i, 0))
    return pl.pallas_call(
        functools.partial(_sample_kernel, final=final, t_len=t_len),
        grid=(n_streams // NB,),
        in_specs=[pl.BlockSpec(memory_space=pltpu.SMEM), x_spec, c_spec, c_spec]
        + _weight_specs(wsp, bsp, bias),
        out_specs=[x_spec, c_spec, c_spec, vg_spec],
        out_shape=[jax.ShapeDtypeStruct(x.shape, F32),
                   jax.ShapeDtypeStruct(ck.shape, F32), jax.ShapeDtypeStruct(cv.shape, F32),
                   jax.ShapeDtypeStruct((rows, D_GMLP), F32)],
        scratch_shapes=[
            pltpu.VMEM((step_rows, D_MODEL), BF16),
            pltpu.VMEM((step_rows, D_ATTN), BF16),
            pltpu.VMEM((step_rows, D_ATTN), F32),
            pltpu.VMEM((step_rows, D_GMLP), F32),
            pltpu.VMEM((step_rows, D_GMLP), F32),
            pltpu.VMEM((step_rows, D_GMLP), BF16),
            pltpu.VMEM((step_rows, D_GMLP), BF16),
            pltpu.VMEM((step_rows, D_ATTN), F32),
            pltpu.VMEM((step_rows, D_MODEL), BF16),
        ],
        compiler_params=pltpu.CompilerParams(
            dimension_semantics=("arbitrary",), vmem_limit_bytes=VMEM_LIMIT),
        name="sample_layer_final" if final else "sample_layer",
    )(sinks, x, ck, cv, w_in, w_out, wsp, bsp, bias, *vecs)


def kernel(x_prompt, x_sample, cache_k, cache_v, w_in, w_out, norm_in, rel_bias, sinks,
           norm_attn, norm_gmlp, ln_v_g, ln_v_b, w_spatial, b_spatial, norm_final):
    bp, seq, _ = x_prompt.shape
    bs, t_len, _ = x_sample.shape
    n_past = cache_k.shape[2]
    assert seq % TM == 0 and TM % PAIR_ROWS == 0 and TM % GMLP_CHUNK == 0 and TM >= WINDOW
    assert bs % NB == 0 and t_len <= GMLP_CHUNK and n_past >= t_len
    assert TM % ROW_BLOCK == 0 and (NB * t_len) % ROW_BLOCK == 0

    bias_p = _prompt_bias_tables(rel_bias)
    bias_s = _sample_bias_table(rel_bias, t_len, n_past)
    w_in_b = jnp.concatenate(
        [w_in[..., :O_GA], _heads_g_major(w_in[..., O_GA:O_U], 2), w_in[..., O_U:]],
        axis=-1).astype(BF16)
    w_out_b = jnp.concatenate(
        [_heads_g_major(w_out[:, :D_ATTN], 1), w_out[:, D_ATTN:]], axis=1).astype(BF16)
    norm_attn_p = _heads_g_major(norm_attn, 1)
    ck = cache_k.reshape(DEPTH, bs, n_past, D_KV)
    cv = cache_v.reshape(DEPTH, bs, n_past, D_KV)

    xp = x_prompt
    xs = x_sample.reshape(bs * t_len, D_MODEL)
    kp, vp, ks, vs, vgs = [], [], [], [], []
    for l in range(DEPTH):
        final = l == DEPTH - 1
        vecs = (norm_in[l][None], norm_attn_p[l][None], norm_gmlp[l][None],
                ln_v_g[l][None], ln_v_b[l][None], norm_final[None])
        wsp_p, bsp_p = _pair_spatial(w_spatial[l], b_spatial[l], GMLP_CHUNK)
        wsp_s, bsp_s = _pair_spatial(w_spatial[l], b_spatial[l], t_len)
        xp, k_l, v_l = _prompt_layer(xp, sinks[l], w_in_b[l], w_out_b[l], wsp_p, bsp_p, bias_p,
                                     vecs, final)
        kp.append(k_l)
        vp.append(v_l)
        xs, nk, nv, vg = _sample_layer(xs, ck[l], cv[l], sinks[l], w_in_b[l], w_out_b[l],
                                       wsp_s, bsp_s, bias_s, vecs, final, t_len)
        ks.append(nk)
        vs.append(nv)
        vgs.append(vg)

    def heads(rows, n):
        return jnp.stack(rows).reshape(DEPTH, n, -1, N_KV_HEADS, HEAD_DIM)

    return (xp, xs.reshape(bs, t_len, D_MODEL),
            heads(kp, bp), heads(vp, bp), heads(ks, bs), heads(vs, bs),
            jnp.stack(vgs).reshape(DEPTH, bs, t_len, N_GROUPS_B, HEAD_DIM))
```

```python
import functools
import math

import jax
import jax.numpy as jnp
from jax import lax
from jax.experimental import pallas as pl
from jax.experimental.pallas import tpu as pltpu

D_MODEL = 2048
DEPTH = 4
CHUNK = 64
HEAD_DIM = 64
D_ATTN = 1024
D_GMLP = 1024
N_HEADS = 16
N_KV_HEADS = 4
GQA_GROUP = 4
D_KV = 256
WINDOW = 128
N_BUCKETS = 32
MAX_DISTANCE = 128
GMLP_CHUNK = 128
N_GROUPS_B = 16
D_PROJ = 5632
EPS = 1e-6
NEG_INF = -1e30

O_Q, O_K, O_V, O_GA, O_U, O_VG, O_GB = 0, 1024, 1280, 1536, 2560, 3584, 4608

LANES = 128
N_PAIRS = N_GROUPS_B // 2
PAIR_ROWS = 2 * CHUNK
PAIR_KEYS = 4 * CHUNK
TM = 256
NB = 4
PROJ_COLS = 512
ROW_BLOCK = 32
VMEM_LIMIT = 58 * 1024 * 1024

F32 = jnp.float32
BF16 = jnp.bfloat16


def _rel_bucket(rel):
    nb = N_BUCKETS // 2
    max_exact = nb // 2
    base = jnp.where(rel > 0, nb, 0)
    n = jnp.abs(rel)
    nf = jnp.maximum(n, 1).astype(F32)
    large = max_exact + (jnp.log(nf / max_exact) / math.log(MAX_DISTANCE / max_exact)
                         * (nb - max_exact)).astype(jnp.int32)
    large = jnp.minimum(large, nb - 1)
    return base + jnp.where(n < max_exact, n, large)


def _prompt_bias_tables(rel_bias):
    r = jnp.arange(PAIR_ROWS)[None, :]
    j = jnp.arange(PAIR_KEYS)[:, None]
    rel = (j - WINDOW) - r
    b = rel_bias[_rel_bucket(rel)].astype(F32)
    qc = r // CHUNK
    kc = j // CHUNK - 2
    in_window = (kc <= qc) & (kc >= qc - 2)
    tabs = []
    for first in (False, True):
        ok = (in_window & (j >= WINDOW)) if first else in_window
        t = jnp.where(ok[:, :, None], b, NEG_INF)
        t = jnp.transpose(t, (2, 0, 1)).reshape(N_KV_HEADS, GQA_GROUP, PAIR_KEYS, PAIR_ROWS)
        t = jnp.transpose(t, (0, 2, 1, 3)).reshape(N_KV_HEADS, PAIR_KEYS, GQA_GROUP * PAIR_ROWS)
        tabs.append(t)
    return jnp.stack(tabs)


def _sample_bias_table(rel_bias, n_q, n_past):
    i = jnp.arange(n_q)[:, None]
    j = jnp.arange(n_past + n_q)[None, :]
    b = rel_bias[_rel_bucket(j - n_past - i)].astype(F32)
    return jnp.transpose(b, (2, 0, 1)).reshape(N_KV_HEADS, GQA_GROUP * n_q, n_past + n_q)


def _pair_spatial(w_s, b_s, rows):
    depth = w_s.shape[0]
    w = w_s[:, :, :rows, :rows].reshape(depth, N_PAIRS, 2, rows, rows)
    w = jnp.transpose(w, (0, 1, 3, 2, 4)).reshape(depth, N_PAIRS, rows, 2 * rows)
    b = b_s[:, :, :rows].reshape(depth, N_PAIRS, 2, rows)
    b = jnp.repeat(jnp.transpose(b, (0, 1, 3, 2)), LANES // 2, axis=3)
    return w, b


def _heads_g_major(a, axis):
    shp = a.shape
    a = a.reshape(shp[:axis] + (N_KV_HEADS, GQA_GROUP, HEAD_DIM) + shp[axis + 1:])
    return jnp.swapaxes(a, axis, axis + 1).reshape(shp)


def _silu(x):
    return x / (1.0 + jnp.exp(-x))


def _gelu(x):
    return 0.5 * x * (1.0 + lax.erf(x * math.sqrt(0.5)))


def _rms_to_bf16(x_ref, g_ref, h_s):
    for r in range(0, x_ref.shape[0], ROW_BLOCK):
        x = x_ref[r:r + ROW_BLOCK, :]
        inv = lax.rsqrt(jnp.mean(x * x, axis=-1, keepdims=True) + EPS)
        h_s[r:r + ROW_BLOCK, :] = (x * inv * g_ref[...]).astype(BF16)


def _proj(h_s, w_in_ref, off, width):
    return jnp.dot(h_s[...], w_in_ref[:, off:off + width], preferred_element_type=F32)


def _proj_kvq(h_s, w_in_ref, q_s):
    kv = _proj(h_s, w_in_ref, O_K, 2 * D_KV)
    for c in range(0, D_ATTN, PROJ_COLS):
        q_s[:, c:c + PROJ_COLS] = (
            _proj(h_s, w_in_ref, O_Q + c, PROJ_COLS) * (HEAD_DIM ** -0.5)).astype(BF16)
    return kv[:, :D_KV], kv[:, D_KV:]


def _act_chunks(h_s, w_in_ref, vgf_s, u_s, gb_s, ga_s):
    def make(dst, off, act, c):
        def run():
            dst[:, c:c + PROJ_COLS] = act(_proj(h_s, w_in_ref, off + c, PROJ_COLS))
        return run
    return [make(dst, off, act, c)
            for dst, off, act in ((vgf_s, O_VG, _gelu), (u_s, O_U, _gelu),
                                  (gb_s, O_GB, _silu), (ga_s, O_GA, _silu))
            for c in range(0, D_GMLP, PROJ_COLS)]


def _layernorm_split(vgf_s, ln_g_ref, ln_b_ref, vlo_s, vhi_s):
    low = (lax.broadcasted_iota(jnp.int32, (ROW_BLOCK, D_GMLP), 1) % LANES) < (LANES // 2)
    for r in range(0, vgf_s.shape[0], ROW_BLOCK):
        g = vgf_s[r:r + ROW_BLOCK, :]
        gc = g - jnp.mean(g, axis=-1, keepdims=True)
        y = gc * lax.rsqrt(jnp.mean(gc * gc, axis=-1, keepdims=True) + EPS)
        y = y * ln_g_ref[...] + ln_b_ref[...]
        vgf_s[r:r + ROW_BLOCK, :] = y
        vlo_s[r:r + ROW_BLOCK, :] = jnp.where(low, y, 0.0).astype(BF16)
        vhi_s[r:r + ROW_BLOCK, :] = jnp.where(low, 0.0, y).astype(BF16)


def _attend(q_rows, keys, vals, bias, sink_col):
    s = lax.dot_general(q_rows, keys, (((1,), (1,)), ((), ())), preferred_element_type=F32) + bias
    m = jnp.maximum(jnp.max(s, axis=-1, keepdims=True), sink_col)
    e = jnp.exp(s - m)
    l = jnp.sum(e, axis=-1, keepdims=True) + jnp.exp(sink_col - m)
    o = jnp.dot(e.astype(BF16), vals, preferred_element_type=F32)
    return o * (1.0 / l)


def _scores_t(keys, q_rows):
    return lax.dot_general(keys, q_rows, (((1,), (1,)), ((), ())), preferred_element_type=F32)


def _zero_after(row):
    u = lax.bitcast_convert_type(row, jnp.uint32)
    u = lax.shift_right_logical(lax.shift_right_logical(u, jnp.uint32(16)), jnp.uint32(16))
    return lax.bitcast_convert_type(u, F32)


def _softmax_pv_t(s, vals_t, bias_t, sink_row, after_row):
    s = s + bias_t
    m = jnp.maximum(jnp.max(s, axis=0, keepdims=True), sink_row)
    e = jnp.exp(s - m)
    l = jnp.sum(e, axis=0, keepdims=True) + jnp.exp(sink_row - m)
    p = (e + _zero_after(after_row)).astype(BF16)
    o = jnp.dot(vals_t, p, preferred_element_type=F32)
    return o * (1.0 / l)


def _sink_vec(sinks_ref, layer, kvh, n, axis):
    shape = (n, 1) if axis == 0 else (1, n)
    return jnp.concatenate(
        [jnp.full(shape, sinks_ref[layer, kvh * GQA_GROUP + g], F32) for g in range(GQA_GROUP)],
        axis=axis)


def _masked_spatial(wsp_ref, j, rows):
    w = wsp_ref[j]
    i = lax.broadcasted_iota(jnp.int32, w.shape, 0)
    c = lax.broadcasted_iota(jnp.int32, w.shape, 1) % rows
    return jnp.where(c <= i, w, 0.0).astype(BF16)


def _gated_norm(src_s, gate_s, g_ref, y_s, col0):
    rows, width = src_s.shape
    for r in range(0, rows, ROW_BLOCK):
        a = src_s[r:r + ROW_BLOCK, :]
        inv = lax.rsqrt(jnp.mean(a * a, axis=-1, keepdims=True) + EPS)
        y_s[r:r + ROW_BLOCK, col0:col0 + width] = (
            a * inv * g_ref[...] * gate_s[r:r + ROW_BLOCK, :]).astype(BF16)


def _out_gmlp_half(x_ref, w_out_ref, o_ref, y_s, col_lo=0, col_hi=D_MODEL):
    for c in range(col_lo, col_hi, PROJ_COLS):
        o_ref[:, c:c + PROJ_COLS] = x_ref[:, c:c + PROJ_COLS] + jnp.dot(
            y_s[:, D_ATTN:], w_out_ref[D_ATTN:, c:c + PROJ_COLS], preferred_element_type=F32)


def _out_attn_half(w_out_ref, g_fin_ref, o_ref, y_s, final):
    rows = o_ref.shape[0]
    for c in range(0, D_MODEL, PROJ_COLS):
        o_ref[:, c:c + PROJ_COLS] = o_ref[:, c:c + PROJ_COLS] + jnp.dot(
            y_s[:, :D_ATTN], w_out_ref[:D_ATTN, c:c + PROJ_COLS], preferred_element_type=F32)
    if final:
        for r in range(0, rows, ROW_BLOCK):
            xn = o_ref[r:r + ROW_BLOCK, :]
            inv = lax.rsqrt(jnp.mean(xn * xn, axis=-1, keepdims=True) + EPS)
            o_ref[r:r + ROW_BLOCK, :] = xn * inv * g_fin_ref[...]


def _prompt_kernel(sinks_ref, x_ref, w_in_ref, w_out_ref, wsp_ref, bsp_ref, bias_ref,
                   g_in_ref, g_attn_ref, g_gmlp_ref, ln_g_ref, ln_b_ref, g_fin_ref,
                   o_ref, k_out_ref, v_out_ref,
                   h_s, q_s, k_ext, vt_ext, ga_s, gb_s, u_s, vgf_s, vlo_s, vhi_s, ao_s, y_s,
                   *, layer):
    final = layer == DEPTH - 1
    t = pl.program_id(1)

    @pl.when(t == 0)
    def _():
        k_ext[0:WINDOW, :] = jnp.zeros((WINDOW, D_KV), BF16)
        vt_ext[:, 0:WINDOW] = jnp.zeros((D_KV, WINDOW), BF16)

    _rms_to_bf16(x_ref, g_in_ref, h_s)
    k, v = _proj_kvq(h_s, w_in_ref, q_s)
    k_ext[WINDOW:, :] = k.astype(BF16)
    vt_ext[:, WINDOW:] = v.T.astype(BF16)
    k_out_ref[...] = k[TM - WINDOW:, :]
    v_out_ref[...] = v[TM - WINDOW:, :]

    def gmlp_mix():
        for j in range(N_PAIRS):
            w = _masked_spatial(wsp_ref, j, GMLP_CHUNK)
            cols = slice(j * LANES, (j + 1) * LANES)
            for c in range(TM // GMLP_CHUNK):
                rws = slice(c * GMLP_CHUNK, (c + 1) * GMLP_CHUNK)
                rhs = jnp.concatenate([vlo_s[rws, cols], vhi_s[rws, cols]], axis=0)
                mix = jnp.dot(w, rhs, preferred_element_type=F32) + bsp_ref[j]
                u_s[rws, cols] = u_s[rws, cols] * mix

    def store_pair(p, o_t):
        o_t = jnp.concatenate(o_t, axis=0)
        for g in range(GQA_GROUP):
            ao_s[p * PAIR_ROWS:(p + 1) * PAIR_ROWS, g * D_KV:(g + 1) * D_KV] = (
                o_t[:, g * PAIR_ROWS:(g + 1) * PAIR_ROWS].T)

    def scores(i):
        p, kvh = divmod(i, N_KV_HEADS)
        r0 = p * PAIR_ROWS
        c0 = kvh * GQA_GROUP * HEAD_DIM
        q_rows = jnp.concatenate(
            [q_s[r0:r0 + PAIR_ROWS, c0 + g * HEAD_DIM:c0 + (g + 1) * HEAD_DIM]
             for g in range(GQA_GROUP)], axis=0)
        return _scores_t(k_ext[r0:r0 + PAIR_KEYS, kvh * HEAD_DIM:(kvh + 1) * HEAD_DIM], q_rows)

    chunks = _act_chunks(h_s, w_in_ref, vgf_s, u_s, gb_s, ga_s)
    n_blocks = (TM // PAIR_ROWS) * N_KV_HEADS
    first = jnp.where(t == 0, 1, 0)
    o_t = []
    s_next = scores(0)
    for i in range(n_blocks):
        p, kvh = divmod(i, N_KV_HEADS)
        r0 = p * PAIR_ROWS
        hc = slice(kvh * HEAD_DIM, (kvh + 1) * HEAD_DIM)
        s = s_next
        if i + 1 < n_blocks:
            s_next = scores(i + 1)
            after_row = s_next[0:1, :]
        chunks[i]()
        if i + 1 == n_blocks:
            _out_gmlp_half(x_ref, w_out_ref, o_ref, y_s, 0, PROJ_COLS)
            after_row = o_ref[0:1, 0:PROJ_COLS]
        bias_t = bias_ref[first, kvh] if p == 0 else bias_ref[0, kvh]
        o_t.append(_softmax_pv_t(s, vt_ext[hc, r0:r0 + PAIR_KEYS], bias_t,
                                 _sink_vec(sinks_ref, layer, kvh, PAIR_ROWS, 1), after_row))
        if i == 1:
            _layernorm_split(vgf_s, ln_g_ref, ln_b_ref, vlo_s, vhi_s)
        if i == 3:
            store_pair(0, o_t)
            o_t = []
            gmlp_mix()
        if i == 5:
            _gated_norm(u_s, gb_s, g_gmlp_ref, y_s, D_ATTN)
    _out_gmlp_half(x_ref, w_out_ref, o_ref, y_s, PROJ_COLS, D_MODEL)
    store_pair(1, o_t)

    k_ext[0:WINDOW, :] = k_ext[TM:TM + WINDOW, :]
    vt_ext[:, 0:WINDOW] = vt_ext[:, TM:TM + WINDOW]

    _gated_norm(ao_s, ga_s, g_attn_ref, y_s, 0)
    _out_attn_half(w_out_ref, g_fin_ref, o_ref, y_s, final)


def _sample_kernel(sinks_ref, x_ref, ck_ref, cv_ref, w_in_ref, w_out_ref, wsp_ref, bsp_ref, bias_ref,
                   g_in_ref, g_attn_ref, g_gmlp_ref, ln_g_ref, ln_b_ref, g_fin_ref,
                   o_ref, nk_ref, nv_ref, vg_ref,
                   h_s, q_s, ga_s, gb_s, u_s, vlo_s, vhi_s, ao_s, y_s,
                   *, layer, t_len):
    final = layer == DEPTH - 1
    n_past = ck_ref.shape[1]
    _rms_to_bf16(x_ref, g_in_ref, h_s)
    for run in _act_chunks(h_s, w_in_ref, vg_ref, u_s, gb_s, ga_s):
        run()
    k, v = _proj_kvq(h_s, w_in_ref, q_s)
    _layernorm_split(vg_ref, ln_g_ref, ln_b_ref, vlo_s, vhi_s)

    nk_ref[:, 0:n_past - t_len, :] = ck_ref[:, t_len:, :]
    nv_ref[:, 0:n_past - t_len, :] = cv_ref[:, t_len:, :]
    nk_ref[:, n_past - t_len:, :] = k.reshape(NB, t_len, D_KV)
    nv_ref[:, n_past - t_len:, :] = v.reshape(NB, t_len, D_KV)

    for j in range(N_PAIRS):
        w = _masked_spatial(wsp_ref, j, t_len)
        cols = slice(j * LANES, (j + 1) * LANES)
        rhs = jnp.concatenate(
            [jnp.concatenate([vlo_s[b * t_len:(b + 1) * t_len, cols],
                              vhi_s[b * t_len:(b + 1) * t_len, cols]], axis=0)
             for b in range(NB)], axis=1)
        mix = jnp.dot(w, rhs, preferred_element_type=F32)
        for b in range(NB):
            rws = slice(b * t_len, (b + 1) * t_len)
            u_s[rws, cols] = u_s[rws, cols] * (mix[:, b * LANES:(b + 1) * LANES] + bsp_ref[j])
    _gated_norm(u_s, gb_s, g_gmlp_ref, y_s, D_ATTN)
    _out_gmlp_half(x_ref, w_out_ref, o_ref, y_s)

    kb = k.astype(BF16)
    vb = v.astype(BF16)
    for b in range(NB):
        r0 = b * t_len
        for kvh in range(N_KV_HEADS):
            c0 = kvh * GQA_GROUP * HEAD_DIM
            hc = slice(kvh * HEAD_DIM, (kvh + 1) * HEAD_DIM)
            q_rows = jnp.concatenate(
                [q_s[r0:r0 + t_len, c0 + g * HEAD_DIM:c0 + (g + 1) * HEAD_DIM]
                 for g in range(GQA_GROUP)], axis=0)
            keys = jnp.concatenate([ck_ref[b, :, hc].astype(BF16), kb[r0:r0 + t_len, hc]], axis=0)
            vals = jnp.concatenate([cv_ref[b, :, hc].astype(BF16), vb[r0:r0 + t_len, hc]], axis=0)
            o = _attend(q_rows, keys, vals, bias_ref[kvh],
                        _sink_vec(sinks_ref, layer, kvh, t_len, 0))
            for g in range(GQA_GROUP):
                cg = g * D_KV + kvh * HEAD_DIM
                ao_s[r0:r0 + t_len, cg:cg + HEAD_DIM] = o[g * t_len:(g + 1) * t_len, :]

    _gated_norm(ao_s, ga_s, g_attn_ref, y_s, 0)
    _out_attn_half(w_out_ref, g_fin_ref, o_ref, y_s, final)


def _resident(shape):
    nd = len(shape)
    return pl.BlockSpec(shape, lambda *_: (0,) * nd, pipeline_mode=pl.Buffered(1))


def _layer_slice(arr, layer):
    tail = arr.shape[1:]
    return pl.BlockSpec((None,) + tail, lambda *_: (layer,) + (0,) * len(tail),
                        pipeline_mode=pl.Buffered(1))


def _weight_specs(layer, w_in, w_out, wsp, bsp, bias, vecs):
    return ([_layer_slice(w_in, layer), _layer_slice(w_out, layer), _layer_slice(wsp, layer),
             _layer_slice(bsp, layer), _resident(bias.shape)]
            + [_layer_slice(vec, layer) for vec in vecs[:-1]] + [_resident(vecs[-1].shape)])


def _prompt_layer(layer, x, sinks, w_in, w_out, wsp, bsp, bias, vecs):
    batch, seq, _ = x.shape
    kv_shape = jax.ShapeDtypeStruct((batch, WINDOW, D_KV), F32)
    kv_spec = pl.BlockSpec((None, WINDOW, D_KV), lambda b, t: (b, 0, 0))
    x_spec = pl.BlockSpec((None, TM, D_MODEL), lambda b, t: (b, t, 0))
    final = layer == DEPTH - 1
    return pl.pallas_call(
        functools.partial(_prompt_kernel, layer=layer),
        grid=(batch, seq // TM),
        in_specs=[pl.BlockSpec(memory_space=pltpu.SMEM), x_spec]
        + _weight_specs(layer, w_in, w_out, wsp, bsp, bias, vecs),
        out_specs=[x_spec, kv_spec, kv_spec],
        out_shape=[jax.ShapeDtypeStruct(x.shape, F32), kv_shape, kv_shape],
        scratch_shapes=[
            pltpu.VMEM((TM, D_MODEL), BF16),
            pltpu.VMEM((TM, D_ATTN), BF16),
            pltpu.VMEM((WINDOW + TM, D_KV), BF16),
            pltpu.VMEM((D_KV, WINDOW + TM), BF16),
            pltpu.VMEM((TM, D_ATTN), F32),
            pltpu.VMEM((TM, D_GMLP), F32),
            pltpu.VMEM((TM, D_GMLP), F32),
            pltpu.VMEM((TM, D_GMLP), F32),
            pltpu.VMEM((TM, D_GMLP), BF16),
            pltpu.VMEM((TM, D_GMLP), BF16),
            pltpu.VMEM((TM, D_ATTN), F32),
            pltpu.VMEM((TM, D_MODEL), BF16),
        ],
        compiler_params=pltpu.CompilerParams(
            dimension_semantics=("arbitrary", "arbitrary"), vmem_limit_bytes=VMEM_LIMIT),
        name="prompt_layer_final" if final else "prompt_layer",
    )(sinks, x, w_in, w_out, wsp, bsp, bias, *vecs)


def _sample_layer(layer, x, ck, cv, sinks, w_in, w_out, wsp, bsp, bias, vecs, t_len):
    rows = x.shape[0]
    _, n_streams, n_past, _ = ck.shape
    step_rows = NB * t_len
    x_spec = pl.BlockSpec((step_rows, D_MODEL), lambda i: (i, 0))
    cin_spec = pl.BlockSpec((None, NB, n_past, D_KV), lambda i: (layer, i, 0, 0))
    cout_spec = pl.BlockSpec((NB, n_past, D_KV), lambda i: (i, 0, 0))
    vg_spec = pl.BlockSpec((step_rows, D_GMLP), lambda i: (i, 0))
    cache_shape = jax.ShapeDtypeStruct((n_streams, n_past, D_KV), F32)
    final = layer == DEPTH - 1
    return pl.pallas_call(
        functools.partial(_sample_kernel, layer=layer, t_len=t_len),
        grid=(n_streams // NB,),
        in_specs=[pl.BlockSpec(memory_space=pltpu.SMEM), x_spec, cin_spec, cin_spec]
        + _weight_specs(layer, w_in, w_out, wsp, bsp, bias, vecs),
        out_specs=[x_spec, cout_spec, cout_spec, vg_spec],
        out_shape=[jax.ShapeDtypeStruct(x.shape, F32), cache_shape, cache_shape,
                   jax.ShapeDtypeStruct((rows, D_GMLP), F32)],
        scratch_shapes=[
            pltpu.VMEM((step_rows, D_MODEL), BF16),
            pltpu.VMEM((step_rows, D_ATTN), BF16),
            pltpu.VMEM((step_rows, D_ATTN), F32),
            pltpu.VMEM((step_rows, D_GMLP), F32),
            pltpu.VMEM((step_rows, D_GMLP), F32),
            pltpu.VMEM((step_rows, D_GMLP), BF16),
            pltpu.VMEM((step_rows, D_GMLP), BF16),
            pltpu.VMEM((step_rows, D_ATTN), F32),
            pltpu.VMEM((step_rows, D_MODEL), BF16),
        ],
        compiler_params=pltpu.CompilerParams(
            dimension_semantics=("arbitrary",), vmem_limit_bytes=VMEM_LIMIT),
        name="sample_layer_final" if final else "sample_layer",
    )(sinks, x, ck, cv, w_in, w_out, wsp, bsp, bias, *vecs)


def kernel(x_prompt, x_sample, cache_k, cache_v, w_in, w_out, norm_in, rel_bias, sinks,
           norm_attn, norm_gmlp, ln_v_g, ln_v_b, w_spatial, b_spatial, norm_final):
    bp, seq, _ = x_prompt.shape
    bs, t_len, _ = x_sample.shape
    n_past = cache_k.shape[2]
    assert seq % TM == 0 and TM % PAIR_ROWS == 0 and TM % GMLP_CHUNK == 0 and TM >= WINDOW
    assert bs % NB == 0 and t_len <= GMLP_CHUNK and n_past >= t_len
    assert TM % ROW_BLOCK == 0 and (NB * t_len) % ROW_BLOCK == 0
    assert (TM // PAIR_ROWS) * N_KV_HEADS == 4 * (D_GMLP // PROJ_COLS)

    bias_p = _prompt_bias_tables(rel_bias)
    bias_s = _sample_bias_table(rel_bias, t_len, n_past)
    w_in_b = jnp.concatenate(
        [w_in[..., :O_GA], _heads_g_major(w_in[..., O_GA:O_U], 2), w_in[..., O_U:]],
        axis=-1).astype(BF16)
    w_out_b = jnp.concatenate(
        [_heads_g_major(w_out[:, :D_ATTN], 1), w_out[:, D_ATTN:]], axis=1).astype(BF16)
    vecs = (norm_in[:, None], _heads_g_major(norm_attn, 1)[:, None], norm_gmlp[:, None],
            ln_v_g[:, None], ln_v_b[:, None], norm_final[None])
    wsp_p, bsp_p = _pair_spatial(w_spatial, b_spatial, GMLP_CHUNK)
    wsp_s, bsp_s = _pair_spatial(w_spatial, b_spatial, t_len)
    ck = cache_k.reshape(DEPTH, bs, n_past, D_KV)
    cv = cache_v.reshape(DEPTH, bs, n_past, D_KV)

    xp = x_prompt
    xs = x_sample.reshape(bs * t_len, D_MODEL)
    kp, vp, ks, vs, vgs = [], [], [], [], []
    for l in range(DEPTH):
        xp, k_l, v_l = _prompt_layer(l, xp, sinks, w_in_b, w_out_b, wsp_p, bsp_p, bias_p, vecs)
        kp.append(k_l)
        vp.append(v_l)
        xs, nk, nv, vg = _sample_layer(l, xs, ck, cv, sinks, w_in_b, w_out_b, wsp_s, bsp_s,
                                       bias_s, vecs, t_len)
        ks.append(nk)
        vs.append(nv)
        vgs.append(vg)

    def heads(rows, n):
        return jnp.stack(rows).reshape(DEPTH, n, -1, N_KV_HEADS, HEAD_DIM)

    return (xp, xs.reshape(bs, t_len, D_MODEL),
            heads(kp, bp), heads(vp, bp), heads(ks, bs), heads(vs, bs),
            jnp.stack(vgs).reshape(DEPTH, bs, t_len, N_GROUPS_B, HEAD_DIM))
```

```python
import functools
import math

import jax
import jax.numpy as jnp
from jax import lax
from jax.experimental import pallas as pl
from jax.experimental.pallas import tpu as pltpu

D_MODEL = 2048
DEPTH = 4
CHUNK = 64
HEAD_DIM = 64
D_ATTN = 1024
D_GMLP = 1024
N_HEADS = 16
N_KV_HEADS = 4
GQA_GROUP = 4
D_KV = 256
WINDOW = 128
N_BUCKETS = 32
MAX_DISTANCE = 128
GMLP_CHUNK = 128
N_GROUPS_B = 16
D_PROJ = 5632
EPS = 1e-6
NEG_INF = -1e30

O_Q, O_K, O_V, O_GA, O_U, O_VG, O_GB = 0, 1024, 1280, 1536, 2560, 3584, 4608

LANES = 128
BF16_ROWS = 16
N_PAIRS = N_GROUPS_B // 2
PAIR_ROWS = 2 * CHUNK
PAIR_KEYS = 4 * CHUNK
TM = 256
NB = 4
PROJ_COLS = 512
ROW_BLOCK = 32
VMEM_LIMIT = 58 * 1024 * 1024

F32 = jnp.float32
BF16 = jnp.bfloat16


def _rel_bucket(rel):
    nb = N_BUCKETS // 2
    max_exact = nb // 2
    base = jnp.where(rel > 0, nb, 0)
    n = jnp.abs(rel)
    nf = jnp.maximum(n, 1).astype(F32)
    large = max_exact + (jnp.log(nf / max_exact) / math.log(MAX_DISTANCE / max_exact)
                         * (nb - max_exact)).astype(jnp.int32)
    large = jnp.minimum(large, nb - 1)
    return base + jnp.where(n < max_exact, n, large)


def _band_bias(rel_bias, n_q, n_k, n_past):
    n = n_q + n_k - 1
    rel = jnp.arange(n) - (n_q - 1) - n_past
    v = rel_bias[_rel_bucket(rel)].astype(F32).T
    w = jnp.concatenate([v, jnp.zeros((N_HEADS, 1), F32)], axis=1)
    band = jnp.tile(w, (1, n_q))[:, :n_q * n].reshape(N_HEADS, n_q, n)
    return band[:, :, n_q - 1:]


def _prompt_bias_tables(rel_bias):
    b = _band_bias(rel_bias, PAIR_ROWS, PAIR_KEYS, WINDOW)
    b = b.reshape(N_KV_HEADS, GQA_GROUP, PAIR_ROWS, PAIR_KEYS).transpose(0, 3, 1, 2)
    r = jnp.arange(PAIR_ROWS)[None, :]
    j = jnp.arange(PAIR_KEYS)[:, None]
    qc = r // CHUNK
    kc = j // CHUNK - 2
    in_window = (kc <= qc) & (kc >= qc - 2)
    tabs = []
    for ok in (in_window, in_window & (j >= WINDOW)):
        t = jnp.where(ok[None, :, None, :], b, NEG_INF)
        tabs.append(t.reshape(N_KV_HEADS, PAIR_KEYS, GQA_GROUP * PAIR_ROWS))
    return jnp.stack(tabs)


def _sample_bias_table(rel_bias, n_q, n_past):
    b = _band_bias(rel_bias, n_q, n_past + n_q, n_past)
    return b.reshape(N_KV_HEADS, GQA_GROUP * n_q, n_past + n_q)


def _pair_spatial(w_s, b_s, rows):
    depth = w_s.shape[0]
    w = w_s[:, :, :rows, :rows].reshape(depth, N_PAIRS, 2, rows, rows)
    w = jnp.transpose(w, (0, 1, 3, 2, 4)).reshape(depth, N_PAIRS, rows, 2 * rows)
    b = b_s[:, :, :rows].reshape(depth, N_PAIRS, 2, rows)
    b = jnp.repeat(jnp.transpose(b, (0, 1, 3, 2)), LANES // 2, axis=3)
    return w, b


def _silu(x):
    return x / (1.0 + jnp.exp(-x))


def _gelu(x):
    return 0.5 * x * (1.0 + lax.erf(x * math.sqrt(0.5)))


def _rms_to_bf16(x_ref, g_ref, h_s):
    for r in range(0, x_ref.shape[0], ROW_BLOCK):
        x = x_ref[r:r + ROW_BLOCK, :]
        inv = lax.rsqrt(jnp.mean(x * x, axis=-1, keepdims=True) + EPS)
        h_s[r:r + ROW_BLOCK, :] = (x * inv * g_ref[...]).astype(BF16)


def _proj(h_s, w_in_ref, off, width):
    return jnp.dot(h_s[...], w_in_ref[:, off:off + width], preferred_element_type=F32)


def _proj_kvq(h_s, w_in_ref, q_s):
    kv = _proj(h_s, w_in_ref, O_K, 2 * D_KV)
    for c in range(0, D_ATTN, PROJ_COLS):
        q_s[:, c:c + PROJ_COLS] = (
            _proj(h_s, w_in_ref, O_Q + c, PROJ_COLS) * (HEAD_DIM ** -0.5)).astype(BF16)
    return kv[:, :D_KV], kv[:, D_KV:]


def _act_chunks(h_s, w_in_ref, vgf_s, u_s, gb_s, ga_s):
    def make(dst, off, act, c):
        def run():
            dst[:, c:c + PROJ_COLS] = act(_proj(h_s, w_in_ref, off + c, PROJ_COLS))
        return run
    return [make(dst, off, act, c)
            for dst, off, act in ((vgf_s, O_VG, _gelu), (u_s, O_U, _gelu),
                                  (gb_s, O_GB, _silu), (ga_s, O_GA, _silu))
            for c in range(0, D_GMLP, PROJ_COLS)]


def _layernorm_split(vgf_s, ln_g_ref, ln_b_ref, vlo_s, vhi_s):
    low = (lax.broadcasted_iota(jnp.int32, (ROW_BLOCK, D_GMLP), 1) % LANES) < (LANES // 2)
    for r in range(0, vgf_s.shape[0], ROW_BLOCK):
        g = vgf_s[r:r + ROW_BLOCK, :]
        gc = g - jnp.mean(g, axis=-1, keepdims=True)
        y = gc * lax.rsqrt(jnp.mean(gc * gc, axis=-1, keepdims=True) + EPS)
        y = y * ln_g_ref[...] + ln_b_ref[...]
        vgf_s[r:r + ROW_BLOCK, :] = y
        vlo_s[r:r + ROW_BLOCK, :] = jnp.where(low, y, 0.0).astype(BF16)
        vhi_s[r:r + ROW_BLOCK, :] = jnp.where(low, 0.0, y).astype(BF16)


def _attend(q_rows, keys, vals, bias, sink_col):
    s = lax.dot_general(q_rows, keys, (((1,), (1,)), ((), ())), preferred_element_type=F32) + bias
    m = jnp.maximum(jnp.max(s, axis=-1, keepdims=True), sink_col)
    e = jnp.exp(s - m)
    l = jnp.sum(e, axis=-1, keepdims=True) + jnp.exp(sink_col - m)
    o = jnp.dot(e.astype(BF16), vals, preferred_element_type=F32)
    return o * (1.0 / l)


def _scores_t(keys, q_rows):
    return lax.dot_general(keys, q_rows, (((1,), (1,)), ((), ())), preferred_element_type=F32)


def _zero_after(row):
    u = lax.bitcast_convert_type(row, jnp.uint32)
    u = lax.shift_right_logical(lax.shift_right_logical(u, jnp.uint32(16)), jnp.uint32(16))
    return lax.bitcast_convert_type(u, F32)


def _softmax_pv_t(s, vals_t, bias_t, sink_row, after_row):
    s = s + bias_t
    m = jnp.maximum(jnp.max(s, axis=0, keepdims=True), sink_row)
    e = jnp.exp(s - m)
    l = jnp.sum(e, axis=0, keepdims=True) + jnp.exp(sink_row - m)
    tail = e.shape[0] - BF16_ROWS
    p = jnp.concatenate([e[:tail].astype(BF16),
                         (e[tail:] + _zero_after(after_row)).astype(BF16)], axis=0)
    o = jnp.dot(vals_t, p, preferred_element_type=F32)
    return o * (1.0 / l)


def _sink_vec(sinks_ref, layer, kvh, n, axis):
    shape = (n, 1) if axis == 0 else (1, n)
    return jnp.concatenate(
        [jnp.full(shape, sinks_ref[layer, kvh * GQA_GROUP + g], F32) for g in range(GQA_GROUP)],
        axis=axis)


def _masked_spatial(wsp_ref, j, rows):
    w = wsp_ref[j]
    i = lax.broadcasted_iota(jnp.int32, w.shape, 0)
    c = lax.broadcasted_iota(jnp.int32, w.shape, 1) % rows
    return jnp.where(c <= i, w, 0.0).astype(BF16)


def _gated_norm(src_s, gate_s, g_ref, y_s, col0):
    rows, width = src_s.shape
    for r in range(0, rows, ROW_BLOCK):
        a = src_s[r:r + ROW_BLOCK, :]
        inv = lax.rsqrt(jnp.mean(a * a, axis=-1, keepdims=True) + EPS)
        y_s[r:r + ROW_BLOCK, col0:col0 + width] = (
            a * inv * g_ref[...] * gate_s[r:r + ROW_BLOCK, :]).astype(BF16)


def _out_gmlp_half(x_ref, w_out_ref, o_ref, y_s, col_lo=0, col_hi=D_MODEL):
    for c in range(col_lo, col_hi, PROJ_COLS):
        o_ref[:, c:c + PROJ_COLS] = x_ref[:, c:c + PROJ_COLS] + jnp.dot(
            y_s[:, D_ATTN:], w_out_ref[D_ATTN:, c:c + PROJ_COLS], preferred_element_type=F32)


def _out_attn_half(w_out_ref, g_fin_ref, o_ref, y_s, final):
    rows = o_ref.shape[0]
    for c in range(0, D_MODEL, PROJ_COLS):
        o_ref[:, c:c + PROJ_COLS] = o_ref[:, c:c + PROJ_COLS] + jnp.dot(
            y_s[:, :D_ATTN], w_out_ref[:D_ATTN, c:c + PROJ_COLS], preferred_element_type=F32)
    if final:
        for r in range(0, rows, ROW_BLOCK):
            xn = o_ref[r:r + ROW_BLOCK, :]
            inv = lax.rsqrt(jnp.mean(xn * xn, axis=-1, keepdims=True) + EPS)
            o_ref[r:r + ROW_BLOCK, :] = xn * inv * g_fin_ref[...]


def _prompt_kernel(sinks_ref, x_ref, w_in_ref, w_out_ref, wsp_ref, bsp_ref, bias_ref,
                   g_in_ref, g_attn_ref, g_gmlp_ref, ln_g_ref, ln_b_ref, g_fin_ref,
                   o_ref, k_out_ref, v_out_ref,
                   h_s, q_s, k_ext, vt_ext, ga_s, gb_s, u_s, vgf_s, vlo_s, vhi_s, ao_s, y_s,
                   *, layer):
    final = layer == DEPTH - 1
    t = pl.program_id(1)

    @pl.when(t == 0)
    def _():
        k_ext[0:WINDOW, :] = jnp.zeros((WINDOW, D_KV), BF16)
        vt_ext[:, 0:WINDOW] = jnp.zeros((D_KV, WINDOW), BF16)

    _rms_to_bf16(x_ref, g_in_ref, h_s)
    k, v = _proj_kvq(h_s, w_in_ref, q_s)
    k_ext[WINDOW:, :] = k.astype(BF16)
    vt_ext[:, WINDOW:] = v.T.astype(BF16)
    k_out_ref[...] = k[TM - WINDOW:, :]
    v_out_ref[...] = v[TM - WINDOW:, :]

    def gmlp_mix():
        for j in range(N_PAIRS):
            w = _masked_spatial(wsp_ref, j, GMLP_CHUNK)
            cols = slice(j * LANES, (j + 1) * LANES)
            for c in range(TM // GMLP_CHUNK):
                rws = slice(c * GMLP_CHUNK, (c + 1) * GMLP_CHUNK)
                rhs = jnp.concatenate([vlo_s[rws, cols], vhi_s[rws, cols]], axis=0)
                mix = jnp.dot(w, rhs, preferred_element_type=F32) + bsp_ref[j]
                u_s[rws, cols] = u_s[rws, cols] * mix

    def store_block(p, kvh, o_t):
        o_gd = jnp.concatenate(
            [o_t[:, g * PAIR_ROWS:(g + 1) * PAIR_ROWS] for g in range(GQA_GROUP)], axis=0)
        ao_s[p * PAIR_ROWS:(p + 1) * PAIR_ROWS, kvh * D_KV:(kvh + 1) * D_KV] = o_gd.T

    def scores(i):
        p, kvh = divmod(i, N_KV_HEADS)
        r0 = p * PAIR_ROWS
        c0 = kvh * GQA_GROUP * HEAD_DIM
        q_rows = jnp.concatenate(
            [q_s[r0:r0 + PAIR_ROWS, c0 + g * HEAD_DIM:c0 + (g + 1) * HEAD_DIM]
             for g in range(GQA_GROUP)], axis=0)
        return _scores_t(k_ext[r0:r0 + PAIR_KEYS, kvh * HEAD_DIM:(kvh + 1) * HEAD_DIM], q_rows)

    chunks = _act_chunks(h_s, w_in_ref, vgf_s, u_s, gb_s, ga_s)
    n_blocks = (TM // PAIR_ROWS) * N_KV_HEADS
    first = jnp.where(t == 0, 1, 0)
    chunks[0]()
    s_next = scores(0)
    for i in range(n_blocks):
        p, kvh = divmod(i, N_KV_HEADS)
        r0 = p * PAIR_ROWS
        hc = slice(kvh * HEAD_DIM, (kvh + 1) * HEAD_DIM)
        s = s_next
        if i + 1 < n_blocks:
            s_next = scores(i + 1)
            after_row = s_next[0:1, :]
            chunks[i + 1]()
        else:
            _out_gmlp_half(x_ref, w_out_ref, o_ref, y_s, 0, PROJ_COLS)
            after_row = o_ref[0:1, 0:PROJ_COLS]
        bias_t = bias_ref[first, kvh] if p == 0 else bias_ref[0, kvh]
        store_block(p, kvh, _softmax_pv_t(
            s, vt_ext[hc, r0:r0 + PAIR_KEYS], bias_t,
            _sink_vec(sinks_ref, layer, kvh, PAIR_ROWS, 1), after_row))
        if i == 0:
            _layernorm_split(vgf_s, ln_g_ref, ln_b_ref, vlo_s, vhi_s)
        if i == 2:
            gmlp_mix()
        if i == 4:
            _gated_norm(u_s, gb_s, g_gmlp_ref, y_s, D_ATTN)
    _out_gmlp_half(x_ref, w_out_ref, o_ref, y_s, PROJ_COLS, D_MODEL)

    k_ext[0:WINDOW, :] = k_ext[TM:TM + WINDOW, :]
    vt_ext[:, 0:WINDOW] = vt_ext[:, TM:TM + WINDOW]

    _gated_norm(ao_s, ga_s, g_attn_ref, y_s, 0)
    _out_attn_half(w_out_ref, g_fin_ref, o_ref, y_s, final)


def _sample_kernel(sinks_ref, x_ref, ck_ref, cv_ref, w_in_ref, w_out_ref, wsp_ref, bsp_ref, bias_ref,
                   g_in_ref, g_attn_ref, g_gmlp_ref, ln_g_ref, ln_b_ref, g_fin_ref,
                   o_ref, nk_ref, nv_ref, vg_ref,
                   h_s, q_s, ga_s, gb_s, u_s, vlo_s, vhi_s, ao_s, y_s,
                   *, layer, t_len):
    final = layer == DEPTH - 1
    n_past = ck_ref.shape[1]
    _rms_to_bf16(x_ref, g_in_ref, h_s)
    for run in _act_chunks(h_s, w_in_ref, vg_ref, u_s, gb_s, ga_s):
        run()
    k, v = _proj_kvq(h_s, w_in_ref, q_s)
    _layernorm_split(vg_ref, ln_g_ref, ln_b_ref, vlo_s, vhi_s)

    nk_ref[:, 0:n_past - t_len, :] = ck_ref[:, t_len:, :]
    nv_ref[:, 0:n_past - t_len, :] = cv_ref[:, t_len:, :]
    nk_ref[:, n_past - t_len:, :] = k.reshape(NB, t_len, D_KV)
    nv_ref[:, n_past - t_len:, :] = v.reshape(NB, t_len, D_KV)

    for j in range(N_PAIRS):
        w = _masked_spatial(wsp_ref, j, t_len)
        cols = slice(j * LANES, (j + 1) * LANES)
        rhs = jnp.concatenate(
            [jnp.concatenate([vlo_s[b * t_len:(b + 1) * t_len, cols],
                              vhi_s[b * t_len:(b + 1) * t_len, cols]], axis=0)
             for b in range(NB)], axis=1)
        mix = jnp.dot(w, rhs, preferred_element_type=F32)
        for b in range(NB):
            rws = slice(b * t_len, (b + 1) * t_len)
            u_s[rws, cols] = u_s[rws, cols] * (mix[:, b * LANES:(b + 1) * LANES] + bsp_ref[j])
    _gated_norm(u_s, gb_s, g_gmlp_ref, y_s, D_ATTN)
    _out_gmlp_half(x_ref, w_out_ref, o_ref, y_s)

    kb = k.astype(BF16)
    vb = v.astype(BF16)
    for b in range(NB):
        r0 = b * t_len
        for kvh in range(N_KV_HEADS):
            c0 = kvh * GQA_GROUP * HEAD_DIM
            hc = slice(kvh * HEAD_DIM, (kvh + 1) * HEAD_DIM)
            q_rows = jnp.concatenate(
                [q_s[r0:r0 + t_len, c0 + g * HEAD_DIM:c0 + (g + 1) * HEAD_DIM]
                 for g in range(GQA_GROUP)], axis=0)
            keys = jnp.concatenate([ck_ref[b, :, hc].astype(BF16), kb[r0:r0 + t_len, hc]], axis=0)
            vals = jnp.concatenate([cv_ref[b, :, hc].astype(BF16), vb[r0:r0 + t_len, hc]], axis=0)
            o = _attend(q_rows, keys, vals, bias_ref[kvh],
                        _sink_vec(sinks_ref, layer, kvh, t_len, 0))
            for g in range(GQA_GROUP):
                cg = c0 + g * HEAD_DIM
                ao_s[r0:r0 + t_len, cg:cg + HEAD_DIM] = o[g * t_len:(g + 1) * t_len, :]

    _gated_norm(ao_s, ga_s, g_attn_ref, y_s, 0)
    _out_attn_half(w_out_ref, g_fin_ref, o_ref, y_s, final)


def _resident(shape):
    nd = len(shape)
    return pl.BlockSpec(shape, lambda *_: (0,) * nd, pipeline_mode=pl.Buffered(1))


def _layer_slice(arr, layer):
    tail = arr.shape[1:]
    return pl.BlockSpec((None,) + tail, lambda *_: (layer,) + (0,) * len(tail),
                        pipeline_mode=pl.Buffered(1))


def _weight_specs(layer, w_in, w_out, wsp, bsp, bias, vecs):
    return ([_layer_slice(w_in, layer), _layer_slice(w_out, layer), _layer_slice(wsp, layer),
             _layer_slice(bsp, layer), _resident(bias.shape)]
            + [_layer_slice(vec, layer) for vec in vecs[:-1]] + [_resident(vecs[-1].shape)])


def _prompt_layer(layer, x, sinks, w_in, w_out, wsp, bsp, bias, vecs):
    batch, seq, _ = x.shape
    kv_shape = jax.ShapeDtypeStruct((batch, WINDOW, D_KV), F32)
    kv_spec = pl.BlockSpec((None, WINDOW, D_KV), lambda b, t: (b, 0, 0))
    x_spec = pl.BlockSpec((None, TM, D_MODEL), lambda b, t: (b, t, 0))
    final = layer == DEPTH - 1
    return pl.pallas_call(
        functools.partial(_prompt_kernel, layer=layer),
        grid=(batch, seq // TM),
        in_specs=[pl.BlockSpec(memory_space=pltpu.SMEM), x_spec]
        + _weight_specs(layer, w_in, w_out, wsp, bsp, bias, vecs),
        out_specs=[x_spec, kv_spec, kv_spec],
        out_shape=[jax.ShapeDtypeStruct(x.shape, F32), kv_shape, kv_shape],
        scratch_shapes=[
            pltpu.VMEM((TM, D_MODEL), BF16),
            pltpu.VMEM((TM, D_ATTN), BF16),
            pltpu.VMEM((WINDOW + TM, D_KV), BF16),
            pltpu.VMEM((D_KV, WINDOW + TM), BF16),
            pltpu.VMEM((TM, D_ATTN), F32),
            pltpu.VMEM((TM, D_GMLP), F32),
            pltpu.VMEM((TM, D_GMLP), F32),
            pltpu.VMEM((TM, D_GMLP), F32),
            pltpu.VMEM((TM, D_GMLP), BF16),
            pltpu.VMEM((TM, D_GMLP), BF16),
            pltpu.VMEM((TM, D_ATTN), F32),
            pltpu.VMEM((TM, D_MODEL), BF16),
        ],
        compiler_params=pltpu.CompilerParams(
            dimension_semantics=("arbitrary", "arbitrary"), vmem_limit_bytes=VMEM_LIMIT),
        name="prompt_layer_final" if final else "prompt_layer",
    )(sinks, x, w_in, w_out, wsp, bsp, bias, *vecs)


def _sample_layer(layer, x, ck, cv, sinks, w_in, w_out, wsp, bsp, bias, vecs, t_len):
    rows = x.shape[0]
    _, n_streams, n_past, _ = ck.shape
    step_rows = NB * t_len
    x_spec = pl.BlockSpec((step_rows, D_MODEL), lambda i: (i, 0))
    cin_spec = pl.BlockSpec((None, NB, n_past, D_KV), lambda i: (layer, i, 0, 0))
    cout_spec = pl.BlockSpec((NB, n_past, D_KV), lambda i: (i, 0, 0))
    vg_spec = pl.BlockSpec((step_rows, D_GMLP), lambda i: (i, 0))
    cache_shape = jax.ShapeDtypeStruct((n_streams, n_past, D_KV), F32)
    final = layer == DEPTH - 1
    return pl.pallas_call(
        functools.partial(_sample_kernel, layer=layer, t_len=t_len),
        grid=(n_streams // NB,),
        in_specs=[pl.BlockSpec(memory_space=pltpu.SMEM), x_spec, cin_spec, cin_spec]
        + _weight_specs(layer, w_in, w_out, wsp, bsp, bias, vecs),
        out_specs=[x_spec, cout_spec, cout_spec, vg_spec],
        out_shape=[jax.ShapeDtypeStruct(x.shape, F32), cache_shape, cache_shape,
                   jax.ShapeDtypeStruct((rows, D_GMLP), F32)],
        scratch_shapes=[
            pltpu.VMEM((step_rows, D_MODEL), BF16),
            pltpu.VMEM((step_rows, D_ATTN), BF16),
            pltpu.VMEM((step_rows, D_ATTN), F32),
            pltpu.VMEM((step_rows, D_GMLP), F32),
            pltpu.VMEM((step_rows, D_GMLP), F32),
            pltpu.VMEM((step_rows, D_GMLP), BF16),
            pltpu.VMEM((step_rows, D_GMLP), BF16),
            pltpu.VMEM((step_rows, D_ATTN), F32),
            pltpu.VMEM((step_rows, D_MODEL), BF16),
        ],
        compiler_params=pltpu.CompilerParams(
            dimension_semantics=("arbitrary",), vmem_limit_bytes=VMEM_LIMIT),
        name="sample_layer_final" if final else "sample_layer",
    )(sinks, x, ck, cv, w_in, w_out, wsp, bsp, bias, *vecs)


def kernel(x_prompt, x_sample, cache_k, cache_v, w_in, w_out, norm_in, rel_bias, sinks,
           norm_attn, norm_gmlp, ln_v_g, ln_v_b, w_spatial, b_spatial, norm_final):
    bp, seq, _ = x_prompt.shape
    bs, t_len, _ = x_sample.shape
    n_past = cache_k.shape[2]
    assert seq % TM == 0 and TM % PAIR_ROWS == 0 and TM % GMLP_CHUNK == 0 and TM >= WINDOW
    assert bs % NB == 0 and t_len <= GMLP_CHUNK and n_past >= t_len
    assert TM % ROW_BLOCK == 0 and (NB * t_len) % ROW_BLOCK == 0
    assert (TM // PAIR_ROWS) * N_KV_HEADS == 4 * (D_GMLP // PROJ_COLS)

    bias_p = _prompt_bias_tables(rel_bias)
    bias_s = _sample_bias_table(rel_bias, t_len, n_past)
    w_in_b = w_in.astype(BF16)
    w_out_b = w_out.astype(BF16)
    vecs = (norm_in[:, None], norm_attn[:, None], norm_gmlp[:, None],
            ln_v_g[:, None], ln_v_b[:, None], norm_final[None])
    wsp_p, bsp_p = _pair_spatial(w_spatial, b_spatial, GMLP_CHUNK)
    wsp_s, bsp_s = _pair_spatial(w_spatial, b_spatial, t_len)
    ck = cache_k.reshape(DEPTH, bs, n_past, D_KV)
    cv = cache_v.reshape(DEPTH, bs, n_past, D_KV)

    xp = x_prompt
    xs = x_sample.reshape(bs * t_len, D_MODEL)
    kp, vp, ks, vs, vgs = [], [], [], [], []
    for l in range(DEPTH):
        xp, k_l, v_l = _prompt_layer(l, xp, sinks, w_in_b, w_out_b, wsp_p, bsp_p, bias_p, vecs)
        kp.append(k_l)
        vp.append(v_l)
        xs, nk, nv, vg = _sample_layer(l, xs, ck, cv, sinks, w_in_b, w_out_b, wsp_s, bsp_s,
                                       bias_s, vecs, t_len)
        ks.append(nk)
        vs.append(nv)
        vgs.append(vg)

    def heads(rows, n):
        return jnp.stack(rows).reshape(DEPTH, n, -1, N_KV_HEADS, HEAD_DIM)

    return (xp, xs.reshape(bs, t_len, D_MODEL),
            heads(kp, bp), heads(vp, bp), heads(ks, bs), heads(vs, bs),
            jnp.stack(vgs).reshape(DEPTH, bs, t_len, N_GROUPS_B, HEAD_DIM))
```

```python
import functools
import math

import jax
import jax.numpy as jnp
from jax import lax
from jax.experimental import pallas as pl
from jax.experimental.pallas import tpu as pltpu

D_MODEL = 2048
DEPTH = 4
CHUNK = 64
HEAD_DIM = 64
D_ATTN = 1024
D_GMLP = 1024
N_HEADS = 16
N_KV_HEADS = 4
GQA_GROUP = 4
D_KV = 256
WINDOW = 128
N_BUCKETS = 32
MAX_DISTANCE = 128
GMLP_CHUNK = 128
N_GROUPS_B = 16
D_PROJ = 5632
EPS = 1e-6
NEG_INF = -1e30

O_Q, O_K, O_V, O_GA, O_U, O_VG, O_GB = 0, 1024, 1280, 1536, 2560, 3584, 4608

LANES = 128
BF16_ROWS = 16
N_PAIRS = N_GROUPS_B // 2
PAIR_ROWS = 2 * CHUNK
PAIR_KEYS = 4 * CHUNK
TM = 256
TILES_PER_STEP = 2
NB = 4
PROJ_COLS = 512
ROW_BLOCK = 32
VMEM_LIMIT = 58 * 1024 * 1024

F32 = jnp.float32
BF16 = jnp.bfloat16


def _rel_bucket(rel):
    nb = N_BUCKETS // 2
    max_exact = nb // 2
    base = jnp.where(rel > 0, nb, 0)
    n = jnp.abs(rel)
    nf = jnp.maximum(n, 1).astype(F32)
    large = max_exact + (jnp.log(nf / max_exact) / math.log(MAX_DISTANCE / max_exact)
                         * (nb - max_exact)).astype(jnp.int32)
    large = jnp.minimum(large, nb - 1)
    return base + jnp.where(n < max_exact, n, large)


def _band_bias(rel_bias, n_q, n_k, n_past):
    n = n_q + n_k - 1
    rel = jnp.arange(n) - (n_q - 1) - n_past
    v = rel_bias[_rel_bucket(rel)].astype(F32).T
    w = jnp.concatenate([v, jnp.zeros((N_HEADS, 1), F32)], axis=1)
    band = jnp.tile(w, (1, n_q))[:, :n_q * n].reshape(N_HEADS, n_q, n)
    return band[:, :, n_q - 1:]


def _prompt_bias_table(rel_bias):
    b = _band_bias(rel_bias, PAIR_ROWS, PAIR_KEYS, WINDOW)
    b = b.reshape(N_KV_HEADS, GQA_GROUP, PAIR_ROWS, PAIR_KEYS).transpose(0, 3, 1, 2)
    r = jnp.arange(PAIR_ROWS)[None, :]
    j = jnp.arange(PAIR_KEYS)[:, None]
    qc = r // CHUNK
    kc = j // CHUNK - 2
    in_window = (kc <= qc) & (kc >= qc - 2)
    t = jnp.where(in_window[None, :, None, :], b, NEG_INF)
    return t.reshape(N_KV_HEADS, PAIR_KEYS, GQA_GROUP * PAIR_ROWS)


def _sample_bias_table(rel_bias, n_q, n_past):
    b = _band_bias(rel_bias, n_q, n_past + n_q, n_past)
    return b.reshape(N_KV_HEADS, GQA_GROUP * n_q, n_past + n_q)


def _pair_spatial(w_s, b_s, rows):
    depth = w_s.shape[0]
    w = w_s[:, :, :rows, :rows].reshape(depth, N_PAIRS, 2, rows, rows)
    w = jnp.transpose(w, (0, 1, 3, 2, 4)).reshape(depth, N_PAIRS, rows, 2 * rows)
    b = b_s[:, :, :rows].reshape(depth, N_PAIRS, 2, rows)
    b = jnp.repeat(jnp.transpose(b, (0, 1, 3, 2)), LANES // 2, axis=3)
    return w, b


def _silu(x):
    return x / (1.0 + jnp.exp(-x))


def _gelu(x):
    return 0.5 * x * (1.0 + lax.erf(x * math.sqrt(0.5)))


def _rms_to_bf16(x_ref, g_ref, h_s):
    for r in range(0, x_ref.shape[0], ROW_BLOCK):
        x = x_ref[r:r + ROW_BLOCK, :]
        inv = lax.rsqrt(jnp.mean(x * x, axis=-1, keepdims=True) + EPS)
        h_s[r:r + ROW_BLOCK, :] = (x * inv * g_ref[...]).astype(BF16)


def _proj(h_s, w_in_ref, off, width):
    return jnp.dot(h_s[...], w_in_ref[:, off:off + width], preferred_element_type=F32)


def _proj_kvq(h_s, w_in_ref, q_s):
    kv = _proj(h_s, w_in_ref, O_K, 2 * D_KV)
    for c in range(0, D_ATTN, PROJ_COLS):
        q_s[:, c:c + PROJ_COLS] = (
            _proj(h_s, w_in_ref, O_Q + c, PROJ_COLS) * (HEAD_DIM ** -0.5)).astype(BF16)
    return kv[:, :D_KV], kv[:, D_KV:]


def _act_chunks(h_s, w_in_ref, vgf_s, u_s, gb_s, ga_s):
    def make(dst, off, act, c):
        def run():
            dst[:, c:c + PROJ_COLS] = act(_proj(h_s, w_in_ref, off + c, PROJ_COLS))
        return run
    return [make(dst, off, act, c)
            for dst, off, act in ((vgf_s, O_VG, _gelu), (u_s, O_U, _gelu),
                                  (gb_s, O_GB, _silu), (ga_s, O_GA, _silu))
            for c in range(0, D_GMLP, PROJ_COLS)]


def _layernorm_split(vgf_s, ln_g_ref, ln_b_ref, vlo_s, vhi_s):
    low = (lax.broadcasted_iota(jnp.int32, (ROW_BLOCK, D_GMLP), 1) % LANES) < (LANES // 2)
    for r in range(0, vgf_s.shape[0], ROW_BLOCK):
        g = vgf_s[r:r + ROW_BLOCK, :]
        gc = g - jnp.mean(g, axis=-1, keepdims=True)
        y = gc * lax.rsqrt(jnp.mean(gc * gc, axis=-1, keepdims=True) + EPS)
        y = y * ln_g_ref[...] + ln_b_ref[...]
        vgf_s[r:r + ROW_BLOCK, :] = y
        vlo_s[r:r + ROW_BLOCK, :] = jnp.where(low, y, 0.0).astype(BF16)
        vhi_s[r:r + ROW_BLOCK, :] = jnp.where(low, 0.0, y).astype(BF16)


def _attend(q_rows, keys, vals, bias, sink_col):
    s = lax.dot_general(q_rows, keys, (((1,), (1,)), ((), ())), preferred_element_type=F32) + bias
    m = jnp.maximum(jnp.max(s, axis=-1, keepdims=True), sink_col)
    e = jnp.exp(s - m)
    l = jnp.sum(e, axis=-1, keepdims=True) + jnp.exp(sink_col - m)
    o = jnp.dot(e.astype(BF16), vals, preferred_element_type=F32)
    return o * (1.0 / l)


def _scores_t(keys, q_rows):
    return lax.dot_general(keys, q_rows, (((1,), (1,)), ((), ())), preferred_element_type=F32)


def _zero_after(row):
    u = lax.bitcast_convert_type(row, jnp.uint32)
    u = lax.shift_right_logical(lax.shift_right_logical(u, jnp.uint32(16)), jnp.uint32(16))
    return lax.bitcast_convert_type(u, F32)


def _softmax_pv_t(s, vals_t, bias_t, sink_row, after_row, mask_past=None):
    s = s + bias_t
    if mask_past is not None:
        s = jnp.concatenate([jnp.where(mask_past, NEG_INF, s[:WINDOW]), s[WINDOW:]], axis=0)
    m = jnp.maximum(jnp.max(s, axis=0, keepdims=True), sink_row)
    e = jnp.exp(s - m)
    l = jnp.sum(e, axis=0, keepdims=True) + jnp.exp(sink_row - m)
    tail = e.shape[0] - BF16_ROWS
    p = jnp.concatenate([e[:tail].astype(BF16),
                         (e[tail:] + _zero_after(after_row)).astype(BF16)], axis=0)
    o = jnp.dot(vals_t, p, preferred_element_type=F32)
    return o * (1.0 / l)


def _sink_vec(sinks_ref, layer, kvh, n, axis):
    shape = (n, 1) if axis == 0 else (1, n)
    return jnp.concatenate(
        [jnp.full(shape, sinks_ref[layer, kvh * GQA_GROUP + g], F32) for g in range(GQA_GROUP)],
        axis=axis)


def _masked_spatial(wsp_ref, j, rows):
    w = wsp_ref[j]
    i = lax.broadcasted_iota(jnp.int32, w.shape, 0)
    c = lax.broadcasted_iota(jnp.int32, w.shape, 1) % rows
    return jnp.where(c <= i, w, 0.0).astype(BF16)


def _gated_norm(src_s, gate_s, g_ref, y_s, col0):
    rows, width = src_s.shape
    for r in range(0, rows, ROW_BLOCK):
        a = src_s[r:r + ROW_BLOCK, :]
        inv = lax.rsqrt(jnp.mean(a * a, axis=-1, keepdims=True) + EPS)
        y_s[r:r + ROW_BLOCK, col0:col0 + width] = (
            a * inv * g_ref[...] * gate_s[r:r + ROW_BLOCK, :]).astype(BF16)


def _out_gmlp_half(x_ref, w_out_ref, o_ref, y_s, col_lo=0, col_hi=D_MODEL):
    for c in range(col_lo, col_hi, PROJ_COLS):
        o_ref[:, c:c + PROJ_COLS] = x_ref[:, c:c + PROJ_COLS] + jnp.dot(
            y_s[:, D_ATTN:], w_out_ref[D_ATTN:, c:c + PROJ_COLS], preferred_element_type=F32)


def _out_attn_half(w_out_ref, g_fin_ref, o_ref, y_s, final):
    rows = o_ref.shape[0]
    for c in range(0, D_MODEL, PROJ_COLS):
        o_ref[:, c:c + PROJ_COLS] = o_ref[:, c:c + PROJ_COLS] + jnp.dot(
            y_s[:, :D_ATTN], w_out_ref[:D_ATTN, c:c + PROJ_COLS], preferred_element_type=F32)
    if final:
        for r in range(0, rows, ROW_BLOCK):
            xn = o_ref[r:r + ROW_BLOCK, :]
            inv = lax.rsqrt(jnp.mean(xn * xn, axis=-1, keepdims=True) + EPS)
            o_ref[r:r + ROW_BLOCK, :] = xn * inv * g_fin_ref[...]


def _prompt_tile(sinks_ref, x_ref, w_in_ref, w_out_ref, wsp_ref, bsp_ref, bias_ref,
                 g_in_ref, g_attn_ref, g_gmlp_ref, ln_g_ref, ln_b_ref, g_fin_ref,
                 o_ref, kv_out_refs,
                 h_s, q_s, k_ext, vt_ext, ga_s, gb_s, u_s, vgf_s, vlo_s, vhi_s, ao_s, y_s,
                 *, layer, first):
    final = layer == DEPTH - 1
    _rms_to_bf16(x_ref, g_in_ref, h_s)
    k, v = _proj_kvq(h_s, w_in_ref, q_s)
    k_ext[WINDOW:, :] = k.astype(BF16)
    vt_ext[:, WINDOW:] = v.T.astype(BF16)
    if kv_out_refs is not None:
        kv_out_refs[0][...] = k[TM - WINDOW:, :]
        kv_out_refs[1][...] = v[TM - WINDOW:, :]

    def gmlp_mix():
        for j in range(N_PAIRS):
            w = _masked_spatial(wsp_ref, j, GMLP_CHUNK)
            cols = slice(j * LANES, (j + 1) * LANES)
            for c in range(TM // GMLP_CHUNK):
                rws = slice(c * GMLP_CHUNK, (c + 1) * GMLP_CHUNK)
                rhs = jnp.concatenate([vlo_s[rws, cols], vhi_s[rws, cols]], axis=0)
                mix = jnp.dot(w, rhs, preferred_element_type=F32) + bsp_ref[j]
                u_s[rws, cols] = u_s[rws, cols] * mix

    def store_block(p, kvh, o_t):
        o_gd = jnp.concatenate(
            [o_t[:, g * PAIR_ROWS:(g + 1) * PAIR_ROWS] for g in range(GQA_GROUP)], axis=0)
        ao_s[p * PAIR_ROWS:(p + 1) * PAIR_ROWS, kvh * D_KV:(kvh + 1) * D_KV] = o_gd.T

    def scores(i):
        p, kvh = divmod(i, N_KV_HEADS)
        r0 = p * PAIR_ROWS
        c0 = kvh * GQA_GROUP * HEAD_DIM
        q_rows = jnp.concatenate(
            [q_s[r0:r0 + PAIR_ROWS, c0 + g * HEAD_DIM:c0 + (g + 1) * HEAD_DIM]
             for g in range(GQA_GROUP)], axis=0)
        return _scores_t(k_ext[r0:r0 + PAIR_KEYS, kvh * HEAD_DIM:(kvh + 1) * HEAD_DIM], q_rows)

    chunks = _act_chunks(h_s, w_in_ref, vgf_s, u_s, gb_s, ga_s)
    n_blocks = (TM // PAIR_ROWS) * N_KV_HEADS
    chunks[0]()
    s_next = scores(0)
    for i in range(n_blocks):
        p, kvh = divmod(i, N_KV_HEADS)
        r0 = p * PAIR_ROWS
        hc = slice(kvh * HEAD_DIM, (kvh + 1) * HEAD_DIM)
        s = s_next
        if i + 1 < n_blocks:
            s_next = scores(i + 1)
            after_row = s_next[0:1, :]
            chunks[i + 1]()
        else:
            _out_gmlp_half(x_ref, w_out_ref, o_ref, y_s, 0, PROJ_COLS)
            after_row = o_ref[0:1, 0:PROJ_COLS]
        store_block(p, kvh, _softmax_pv_t(
            s, vt_ext[hc, r0:r0 + PAIR_KEYS], bias_ref[kvh],
            _sink_vec(sinks_ref, layer, kvh, PAIR_ROWS, 1), after_row,
            mask_past=first if p == 0 else None))
        if i == 0:
            _layernorm_split(vgf_s, ln_g_ref, ln_b_ref, vlo_s, vhi_s)
        if i == 2:
            gmlp_mix()
        if i == 4:
            _gated_norm(u_s, gb_s, g_gmlp_ref, y_s, D_ATTN)
    _out_gmlp_half(x_ref, w_out_ref, o_ref, y_s, PROJ_COLS, D_MODEL)

    k_ext[0:WINDOW, :] = k_ext[TM:TM + WINDOW, :]
    vt_ext[:, 0:WINDOW] = vt_ext[:, TM:TM + WINDOW]

    _gated_norm(ao_s, ga_s, g_attn_ref, y_s, 0)
    _out_attn_half(w_out_ref, g_fin_ref, o_ref, y_s, final)


def _prompt_kernel(sinks_ref, x_ref, *refs, layer):
    weights, (o_ref, k_out_ref, v_out_ref), scratch = refs[:11], refs[11:14], refs[14:]
    k_ext, vt_ext = scratch[2], scratch[3]
    t = pl.program_id(1)

    @pl.when(t == 0)
    def _():
        k_ext[0:WINDOW, :] = jnp.zeros((WINDOW, D_KV), BF16)
        vt_ext[:, 0:WINDOW] = jnp.zeros((D_KV, WINDOW), BF16)

    for sub in range(TILES_PER_STEP):
        rows = slice(sub * TM, (sub + 1) * TM)
        last = sub == TILES_PER_STEP - 1
        _prompt_tile(sinks_ref, x_ref.at[rows, :], *weights, o_ref.at[rows, :],
                     (k_out_ref, v_out_ref) if last else None, *scratch,
                     layer=layer, first=(t == 0) if sub == 0 else None)


def _sample_kernel(sinks_ref, x_ref, ck_ref, cv_ref, w_in_ref, w_out_ref, wsp_ref, bsp_ref, bias_ref,
                   g_in_ref, g_attn_ref, g_gmlp_ref, ln_g_ref, ln_b_ref, g_fin_ref,
                   o_ref, nk_ref, nv_ref, vg_ref,
                   h_s, q_s, ga_s, gb_s, u_s, vlo_s, vhi_s, ao_s, y_s,
                   *, layer, t_len):
    final = layer == DEPTH - 1
    n_past = ck_ref.shape[1]
    _rms_to_bf16(x_ref, g_in_ref, h_s)
    for run in _act_chunks(h_s, w_in_ref, vg_ref, u_s, gb_s, ga_s):
        run()
    k, v = _proj_kvq(h_s, w_in_ref, q_s)
    _layernorm_split(vg_ref, ln_g_ref, ln_b_ref, vlo_s, vhi_s)

    nk_ref[:, 0:n_past - t_len, :] = ck_ref[:, t_len:, :]
    nv_ref[:, 0:n_past - t_len, :] = cv_ref[:, t_len:, :]
    nk_ref[:, n_past - t_len:, :] = k.reshape(NB, t_len, D_KV)
    nv_ref[:, n_past - t_len:, :] = v.reshape(NB, t_len, D_KV)

    for j in range(N_PAIRS):
        w = _masked_spatial(wsp_ref, j, t_len)
        cols = slice(j * LANES, (j + 1) * LANES)
        rhs = jnp.concatenate(
            [jnp.concatenate([vlo_s[b * t_len:(b + 1) * t_len, cols],
                              vhi_s[b * t_len:(b + 1) * t_len, cols]], axis=0)
             for b in range(NB)], axis=1)
        mix = jnp.dot(w, rhs, preferred_element_type=F32)
        for b in range(NB):
            rws = slice(b * t_len, (b + 1) * t_len)
            u_s[rws, cols] = u_s[rws, cols] * (mix[:, b * LANES:(b + 1) * LANES] + bsp_ref[j])
    _gated_norm(u_s, gb_s, g_gmlp_ref, y_s, D_ATTN)
    _out_gmlp_half(x_ref, w_out_ref, o_ref, y_s)

    kb = k.astype(BF16)
    vb = v.astype(BF16)
    for b in range(NB):
        r0 = b * t_len
        for kvh in range(N_KV_HEADS):
            c0 = kvh * GQA_GROUP * HEAD_DIM
            hc = slice(kvh * HEAD_DIM, (kvh + 1) * HEAD_DIM)
            q_rows = jnp.concatenate(
                [q_s[r0:r0 + t_len, c0 + g * HEAD_DIM:c0 + (g + 1) * HEAD_DIM]
                 for g in range(GQA_GROUP)], axis=0)
            keys = jnp.concatenate([ck_ref[b, :, hc].astype(BF16), kb[r0:r0 + t_len, hc]], axis=0)
            vals = jnp.concatenate([cv_ref[b, :, hc].astype(BF16), vb[r0:r0 + t_len, hc]], axis=0)
            o = _attend(q_rows, keys, vals, bias_ref[kvh],
                        _sink_vec(sinks_ref, layer, kvh, t_len, 0))
            for g in range(GQA_GROUP):
                cg = c0 + g * HEAD_DIM
                ao_s[r0:r0 + t_len, cg:cg + HEAD_DIM] = o[g * t_len:(g + 1) * t_len, :]

    _gated_norm(ao_s, ga_s, g_attn_ref, y_s, 0)
    _out_attn_half(w_out_ref, g_fin_ref, o_ref, y_s, final)


def _resident(shape):
    nd = len(shape)
    return pl.BlockSpec(shape, lambda *_: (0,) * nd, pipeline_mode=pl.Buffered(1))


def _layer_slice(arr, layer):
    tail = arr.shape[1:]
    return pl.BlockSpec((None,) + tail, lambda *_: (layer,) + (0,) * len(tail),
                        pipeline_mode=pl.Buffered(1))


def _weight_specs(layer, w_in, w_out, wsp, bsp, bias, vecs):
    return ([_layer_slice(w_in, layer), _layer_slice(w_out, layer), _layer_slice(wsp, layer),
             _layer_slice(bsp, layer), _resident(bias.shape)]
            + [_layer_slice(vec, layer) for vec in vecs[:-1]] + [_resident(vecs[-1].shape)])


def _prompt_layer(layer, x, sinks, w_in, w_out, wsp, bsp, bias, vecs):
    batch, seq, _ = x.shape
    kv_shape = jax.ShapeDtypeStruct((batch, WINDOW, D_KV), F32)
    kv_spec = pl.BlockSpec((None, WINDOW, D_KV), lambda b, t: (b, 0, 0))
    x_spec = pl.BlockSpec((None, TILES_PER_STEP * TM, D_MODEL), lambda b, t: (b, t, 0))
    final = layer == DEPTH - 1
    return pl.pallas_call(
        functools.partial(_prompt_kernel, layer=layer),
        grid=(batch, seq // (TILES_PER_STEP * TM)),
        in_specs=[pl.BlockSpec(memory_space=pltpu.SMEM), x_spec]
        + _weight_specs(layer, w_in, w_out, wsp, bsp, bias, vecs),
        out_specs=[x_spec, kv_spec, kv_spec],
        out_shape=[jax.ShapeDtypeStruct(x.shape, F32), kv_shape, kv_shape],
        scratch_shapes=[
            pltpu.VMEM((TM, D_MODEL), BF16),
            pltpu.VMEM((TM, D_ATTN), BF16),
            pltpu.VMEM((WINDOW + TM, D_KV), BF16),
            pltpu.VMEM((D_KV, WINDOW + TM), BF16),
            pltpu.VMEM((TM, D_ATTN), F32),
            pltpu.VMEM((TM, D_GMLP), F32),
            pltpu.VMEM((TM, D_GMLP), F32),
            pltpu.VMEM((TM, D_GMLP), F32),
            pltpu.VMEM((TM, D_GMLP), BF16),
            pltpu.VMEM((TM, D_GMLP), BF16),
            pltpu.VMEM((TM, D_ATTN), F32),
            pltpu.VMEM((TM, D_MODEL), BF16),
        ],
        compiler_params=pltpu.CompilerParams(
            dimension_semantics=("arbitrary", "arbitrary"), vmem_limit_bytes=VMEM_LIMIT),
        name="prompt_layer_final" if final else "prompt_layer",
    )(sinks, x, w_in, w_out, wsp, bsp, bias, *vecs)


def _sample_layer(layer, x, ck, cv, sinks, w_in, w_out, wsp, bsp, bias, vecs, t_len):
    rows = x.shape[0]
    _, n_streams, n_past, _ = ck.shape
    step_rows = NB * t_len
    x_spec = pl.BlockSpec((step_rows, D_MODEL), lambda i: (i, 0))
    cin_spec = pl.BlockSpec((None, NB, n_past, D_KV), lambda i: (layer, i, 0, 0))
    cout_spec = pl.BlockSpec((NB, n_past, D_KV), lambda i: (i, 0, 0))
    vg_spec = pl.BlockSpec((step_rows, D_GMLP), lambda i: (i, 0))
    cache_shape = jax.ShapeDtypeStruct((n_streams, n_past, D_KV), F32)
    final = layer == DEPTH - 1
    return pl.pallas_call(
        functools.partial(_sample_kernel, layer=layer, t_len=t_len),
        grid=(n_streams // NB,),
        in_specs=[pl.BlockSpec(memory_space=pltpu.SMEM), x_spec, cin_spec, cin_spec]
        + _weight_specs(layer, w_in, w_out, wsp, bsp, bias, vecs),
        out_specs=[x_spec, cout_spec, cout_spec, vg_spec],
        out_shape=[jax.ShapeDtypeStruct(x.shape, F32), cache_shape, cache_shape,
                   jax.ShapeDtypeStruct((rows, D_GMLP), F32)],
        scratch_shapes=[
            pltpu.VMEM((step_rows, D_MODEL), BF16),
            pltpu.VMEM((step_rows, D_ATTN), BF16),
            pltpu.VMEM((step_rows, D_ATTN), F32),
            pltpu.VMEM((step_rows, D_GMLP), F32),
            pltpu.VMEM((step_rows, D_GMLP), F32),
            pltpu.VMEM((step_rows, D_GMLP), BF16),
            pltpu.VMEM((step_rows, D_GMLP), BF16),
            pltpu.VMEM((step_rows, D_ATTN), F32),
            pltpu.VMEM((step_rows, D_MODEL), BF16),
        ],
        compiler_params=pltpu.CompilerParams(
            dimension_semantics=("arbitrary",), vmem_limit_bytes=VMEM_LIMIT),
        name="sample_layer_final" if final else "sample_layer",
    )(sinks, x, ck, cv, w_in, w_out, wsp, bsp, bias, *vecs)


def kernel(x_prompt, x_sample, cache_k, cache_v, w_in, w_out, norm_in, rel_bias, sinks,
           norm_attn, norm_gmlp, ln_v_g, ln_v_b, w_spatial, b_spatial, norm_final):
    bp, seq, _ = x_prompt.shape
    bs, t_len, _ = x_sample.shape
    n_past = cache_k.shape[2]
    assert seq % (TILES_PER_STEP * TM) == 0 and TM % PAIR_ROWS == 0 and TM % GMLP_CHUNK == 0 and TM >= WINDOW
    assert bs % NB == 0 and t_len <= GMLP_CHUNK and n_past >= t_len
    assert TM % ROW_BLOCK == 0 and (NB * t_len) % ROW_BLOCK == 0
    assert (TM // PAIR_ROWS) * N_KV_HEADS == 4 * (D_GMLP // PROJ_COLS)

    bias_p = _prompt_bias_table(rel_bias)
    bias_s = _sample_bias_table(rel_bias, t_len, n_past)
    w_in_b = w_in.astype(BF16)
    w_out_b = w_out.astype(BF16)
    vecs = (norm_in[:, None], norm_attn[:, None], norm_gmlp[:, None],
            ln_v_g[:, None], ln_v_b[:, None], norm_final[None])
    wsp_p, bsp_p = _pair_spatial(w_spatial, b_spatial, GMLP_CHUNK)
    wsp_s, bsp_s = _pair_spatial(w_spatial, b_spatial, t_len)
    ck = cache_k.reshape(DEPTH, bs, n_past, D_KV)
    cv = cache_v.reshape(DEPTH, bs, n_past, D_KV)

    xp = x_prompt
    xs = x_sample.reshape(bs * t_len, D_MODEL)
    kp, vp, ks, vs, vgs = [], [], [], [], []
    for l in range(DEPTH):
        xp, k_l, v_l = _prompt_layer(l, xp, sinks, w_in_b, w_out_b, wsp_p, bsp_p, bias_p, vecs)
        kp.append(k_l)
        vp.append(v_l)
        xs, nk, nv, vg = _sample_layer(l, xs, ck, cv, sinks, w_in_b, w_out_b, wsp_s, bsp_s,
                                       bias_s, vecs, t_len)
        ks.append(nk)
        vs.append(nv)
        vgs.append(vg)

    def heads(rows, n):
        return jnp.stack(rows).reshape(DEPTH, n, -1, N_KV_HEADS, HEAD_DIM)

    return (xp, xs.reshape(bs, t_len, D_MODEL),
            heads(kp, bp), heads(vp, bp), heads(ks, bs), heads(vs, bs),
            jnp.stack(vgs).reshape(DEPTH, bs, t_len, N_GROUPS_B, HEAD_DIM))
```

```python
import functools
import math

import jax
import jax.numpy as jnp
from jax import lax
from jax.experimental import pallas as pl
from jax.experimental.pallas import tpu as pltpu

D_MODEL = 2048
DEPTH = 4
CHUNK = 64
HEAD_DIM = 64
D_ATTN = 1024
D_GMLP = 1024
N_HEADS = 16
N_KV_HEADS = 4
GQA_GROUP = 4
D_KV = 256
WINDOW = 128
N_BUCKETS = 32
MAX_DISTANCE = 128
GMLP_CHUNK = 128
N_GROUPS_B = 16
D_PROJ = 5632
EPS = 1e-6
NEG_INF = -1e30

O_Q, O_K, O_V, O_GA, O_U, O_VG, O_GB = 0, 1024, 1280, 1536, 2560, 3584, 4608

LANES = 128
BF16_ROWS = 16
N_PAIRS = N_GROUPS_B // 2
PAIR_ROWS = 2 * CHUNK
PAIR_KEYS = 4 * CHUNK
TM = 256
TILES_PER_STEP = 2
NB = 4
PROJ_COLS = 512
ROW_BLOCK = 32
VMEM_LIMIT = 58 * 1024 * 1024

F32 = jnp.float32
BF16 = jnp.bfloat16


def _rel_bucket(rel):
    nb = N_BUCKETS // 2
    max_exact = nb // 2
    base = jnp.where(rel > 0, nb, 0)
    n = jnp.abs(rel)
    nf = jnp.maximum(n, 1).astype(F32)
    large = max_exact + (jnp.log(nf / max_exact) / math.log(MAX_DISTANCE / max_exact)
                         * (nb - max_exact)).astype(jnp.int32)
    large = jnp.minimum(large, nb - 1)
    return base + jnp.where(n < max_exact, n, large)


def _band_bias(rel_bias, n_q, n_k, n_past):
    n = n_q + n_k - 1
    rel = jnp.arange(n) - (n_q - 1) - n_past
    v = rel_bias[_rel_bucket(rel)].astype(F32).T
    w = jnp.concatenate([v, jnp.zeros((N_HEADS, 1), F32)], axis=1)
    band = jnp.tile(w, (1, n_q))[:, :n_q * n].reshape(N_HEADS, n_q, n)
    return band[:, :, n_q - 1:]


def _prompt_bias_table(rel_bias):
    b = _band_bias(rel_bias, PAIR_ROWS, PAIR_KEYS, WINDOW)
    b = b.reshape(N_KV_HEADS, GQA_GROUP, PAIR_ROWS, PAIR_KEYS).transpose(0, 3, 1, 2)
    r = jnp.arange(PAIR_ROWS)[None, :]
    j = jnp.arange(PAIR_KEYS)[:, None]
    qc = r // CHUNK
    kc = j // CHUNK - 2
    in_window = (kc <= qc) & (kc >= qc - 2)
    t = jnp.where(in_window[None, :, None, :], b, NEG_INF)
    return t.reshape(N_KV_HEADS, PAIR_KEYS, GQA_GROUP * PAIR_ROWS)


def _sample_bias_table(rel_bias, n_q, n_past):
    b = _band_bias(rel_bias, n_q, n_past + n_q, n_past)
    return jnp.swapaxes(b.reshape(N_KV_HEADS, GQA_GROUP * n_q, n_past + n_q), 1, 2)


def _pair_spatial(w_s, b_s, rows):
    depth = w_s.shape[0]
    w = w_s[:, :, :rows, :rows].reshape(depth, N_PAIRS, 2, rows, rows)
    w = jnp.transpose(w, (0, 1, 3, 2, 4)).reshape(depth, N_PAIRS, rows, 2 * rows)
    b = b_s[:, :, :rows].reshape(depth, N_PAIRS, 2, rows)
    b = jnp.repeat(jnp.transpose(b, (0, 1, 3, 2)), LANES // 2, axis=3)
    return w, b


def _silu(x):
    return x / (1.0 + jnp.exp(-x))


def _gelu(x):
    return 0.5 * x * (1.0 + lax.erf(x * math.sqrt(0.5)))


def _rms_to_bf16(x_ref, g_ref, h_s):
    for r in range(0, x_ref.shape[0], ROW_BLOCK):
        x = x_ref[r:r + ROW_BLOCK, :]
        inv = lax.rsqrt(jnp.mean(x * x, axis=-1, keepdims=True) + EPS)
        h_s[r:r + ROW_BLOCK, :] = (x * inv * g_ref[...]).astype(BF16)


def _proj(h_s, w_in_ref, off, width):
    return jnp.dot(h_s[...], w_in_ref[:, off:off + width], preferred_element_type=F32)


def _proj_kvq(h_s, w_in_ref, q_s):
    kv = _proj(h_s, w_in_ref, O_K, 2 * D_KV)
    for c in range(0, D_ATTN, PROJ_COLS):
        q_s[:, c:c + PROJ_COLS] = (
            _proj(h_s, w_in_ref, O_Q + c, PROJ_COLS) * (HEAD_DIM ** -0.5)).astype(BF16)
    return kv[:, :D_KV], kv[:, D_KV:]


def _act_chunks(h_s, w_in_ref, vgf_s, u_s, gb_s, ga_s):
    def make(dst, off, act, c):
        def run():
            dst[:, c:c + PROJ_COLS] = act(_proj(h_s, w_in_ref, off + c, PROJ_COLS))
        return run
    return [make(dst, off, act, c)
            for dst, off, act in ((vgf_s, O_VG, _gelu), (u_s, O_U, _gelu),
                                  (gb_s, O_GB, _silu), (ga_s, O_GA, _silu))
            for c in range(0, D_GMLP, PROJ_COLS)]


def _layernorm_split(vgf_s, ln_g_ref, ln_b_ref, vlo_s, vhi_s):
    low = (lax.broadcasted_iota(jnp.int32, (ROW_BLOCK, D_GMLP), 1) % LANES) < (LANES // 2)
    for r in range(0, vgf_s.shape[0], ROW_BLOCK):
        g = vgf_s[r:r + ROW_BLOCK, :]
        gc = g - jnp.mean(g, axis=-1, keepdims=True)
        y = gc * lax.rsqrt(jnp.mean(gc * gc, axis=-1, keepdims=True) + EPS)
        y = y * ln_g_ref[...] + ln_b_ref[...]
        vgf_s[r:r + ROW_BLOCK, :] = y
        vlo_s[r:r + ROW_BLOCK, :] = jnp.where(low, y, 0.0).astype(BF16)
        vhi_s[r:r + ROW_BLOCK, :] = jnp.where(low, 0.0, y).astype(BF16)


def _scores_t(keys, q_rows):
    return lax.dot_general(keys, q_rows, (((1,), (1,)), ((), ())), preferred_element_type=F32)


def _zero_after(row):
    u = lax.bitcast_convert_type(row, jnp.uint32)
    u = lax.shift_right_logical(lax.shift_right_logical(u, jnp.uint32(16)), jnp.uint32(16))
    return lax.bitcast_convert_type(u, F32)


def _softmax_pv_t(s, vals_t, bias_t, sink_row, after_row, mask_past=None):
    s = s + bias_t
    if mask_past is not None:
        s = jnp.concatenate([jnp.where(mask_past, NEG_INF, s[:WINDOW]), s[WINDOW:]], axis=0)
    m = jnp.maximum(jnp.max(s, axis=0, keepdims=True), sink_row)
    e = jnp.exp(s - m)
    l = jnp.sum(e, axis=0, keepdims=True) + jnp.exp(sink_row - m)
    tail = e.shape[0] - BF16_ROWS
    p = jnp.concatenate([e[:tail].astype(BF16),
                         (e[tail:] + _zero_after(after_row)).astype(BF16)], axis=0)
    o = jnp.dot(vals_t, p, preferred_element_type=F32)
    return o * (1.0 / l)


def _sink_row(sinks_ref, layer, kvh, n):
    shape = (1, GQA_GROUP * n)
    g = lax.broadcasted_iota(jnp.int32, shape, 1) // n
    out = jnp.full(shape, sinks_ref[layer, kvh * GQA_GROUP], F32)
    for i in range(1, GQA_GROUP):
        out = jnp.where(g == i, sinks_ref[layer, kvh * GQA_GROUP + i], out)
    return out


def _masked_spatial(wsp_ref, j, rows):
    w = wsp_ref[j]
    i = lax.broadcasted_iota(jnp.int32, w.shape, 0)
    c = lax.broadcasted_iota(jnp.int32, w.shape, 1) % rows
    return jnp.where(c <= i, w, 0.0).astype(BF16)


def _gated_norm(src_s, gate_s, g_ref, y_s, col0):
    rows, width = src_s.shape
    for r in range(0, rows, ROW_BLOCK):
        a = src_s[r:r + ROW_BLOCK, :]
        inv = lax.rsqrt(jnp.mean(a * a, axis=-1, keepdims=True) + EPS)
        y_s[r:r + ROW_BLOCK, col0:col0 + width] = (
            a * inv * g_ref[...] * gate_s[r:r + ROW_BLOCK, :]).astype(BF16)


def _out_gmlp_half(x_ref, w_out_ref, o_ref, y_s, col_lo=0, col_hi=D_MODEL):
    for c in range(col_lo, col_hi, PROJ_COLS):
        o_ref[:, c:c + PROJ_COLS] = x_ref[:, c:c + PROJ_COLS] + jnp.dot(
            y_s[:, D_ATTN:], w_out_ref[D_ATTN:, c:c + PROJ_COLS], preferred_element_type=F32)


def _out_attn_half(w_out_ref, g_fin_ref, o_ref, y_s, final):
    rows = o_ref.shape[0]
    for c in range(0, D_MODEL, PROJ_COLS):
        o_ref[:, c:c + PROJ_COLS] = o_ref[:, c:c + PROJ_COLS] + jnp.dot(
            y_s[:, :D_ATTN], w_out_ref[:D_ATTN, c:c + PROJ_COLS], preferred_element_type=F32)
    if final:
        for r in range(0, rows, ROW_BLOCK):
            xn = o_ref[r:r + ROW_BLOCK, :]
            inv = lax.rsqrt(jnp.mean(xn * xn, axis=-1, keepdims=True) + EPS)
            o_ref[r:r + ROW_BLOCK, :] = xn * inv * g_fin_ref[...]


def _prompt_tile(sinks_ref, x_ref, w_in_ref, w_out_ref, wsp_ref, bsp_ref, bias_ref,
                 g_in_ref, g_attn_ref, g_gmlp_ref, ln_g_ref, ln_b_ref, g_fin_ref,
                 o_ref, kv_out_refs,
                 h_s, q_s, k_ext, vt_ext, ga_s, gb_s, u_s, vgf_s, vlo_s, vhi_s, ao_s, y_s,
                 *, layer, first):
    final = layer == DEPTH - 1
    _rms_to_bf16(x_ref, g_in_ref, h_s)
    k, v = _proj_kvq(h_s, w_in_ref, q_s)
    k_ext[WINDOW:, :] = k.astype(BF16)
    vt_ext[:, WINDOW:] = v.T.astype(BF16)
    if kv_out_refs is not None:
        kv_out_refs[0][...] = k[TM - WINDOW:, :]
        kv_out_refs[1][...] = v[TM - WINDOW:, :]

    def gmlp_mix():
        for j in range(N_PAIRS):
            w = _masked_spatial(wsp_ref, j, GMLP_CHUNK)
            cols = slice(j * LANES, (j + 1) * LANES)
            for c in range(TM // GMLP_CHUNK):
                rws = slice(c * GMLP_CHUNK, (c + 1) * GMLP_CHUNK)
                rhs = jnp.concatenate([vlo_s[rws, cols], vhi_s[rws, cols]], axis=0)
                mix = jnp.dot(w, rhs, preferred_element_type=F32) + bsp_ref[j]
                u_s[rws, cols] = u_s[rws, cols] * mix

    def store_block(p, kvh, o_t):
        o_gd = jnp.concatenate(
            [o_t[:, g * PAIR_ROWS:(g + 1) * PAIR_ROWS] for g in range(GQA_GROUP)], axis=0)
        ao_s[p * PAIR_ROWS:(p + 1) * PAIR_ROWS, kvh * D_KV:(kvh + 1) * D_KV] = o_gd.T

    def scores(i):
        p, kvh = divmod(i, N_KV_HEADS)
        r0 = p * PAIR_ROWS
        c0 = kvh * GQA_GROUP * HEAD_DIM
        q_rows = jnp.concatenate(
            [q_s[r0:r0 + PAIR_ROWS, c0 + g * HEAD_DIM:c0 + (g + 1) * HEAD_DIM]
             for g in range(GQA_GROUP)], axis=0)
        return _scores_t(k_ext[r0:r0 + PAIR_KEYS, kvh * HEAD_DIM:(kvh + 1) * HEAD_DIM], q_rows)

    chunks = _act_chunks(h_s, w_in_ref, vgf_s, u_s, gb_s, ga_s)
    n_blocks = (TM // PAIR_ROWS) * N_KV_HEADS
    chunks[0]()
    s_next = scores(0)
    for i in range(n_blocks):
        p, kvh = divmod(i, N_KV_HEADS)
        r0 = p * PAIR_ROWS
        hc = slice(kvh * HEAD_DIM, (kvh + 1) * HEAD_DIM)
        s = s_next
        if i + 1 < n_blocks:
            s_next = scores(i + 1)
            after_row = s_next[0:1, :]
            chunks[i + 1]()
        else:
            _out_gmlp_half(x_ref, w_out_ref, o_ref, y_s, 0, PROJ_COLS)
            after_row = o_ref[0:1, 0:PROJ_COLS]
        store_block(p, kvh, _softmax_pv_t(
            s, vt_ext[hc, r0:r0 + PAIR_KEYS], bias_ref[kvh],
            _sink_row(sinks_ref, layer, kvh, PAIR_ROWS), after_row,
            mask_past=first if p == 0 else None))
        if i == 0:
            _layernorm_split(vgf_s, ln_g_ref, ln_b_ref, vlo_s, vhi_s)
        if i == 2:
            gmlp_mix()
        if i == 4:
            _gated_norm(u_s, gb_s, g_gmlp_ref, y_s, D_ATTN)
    _out_gmlp_half(x_ref, w_out_ref, o_ref, y_s, PROJ_COLS, D_MODEL)

    k_ext[0:WINDOW, :] = k_ext[TM:TM + WINDOW, :]
    vt_ext[:, 0:WINDOW] = vt_ext[:, TM:TM + WINDOW]

    _gated_norm(ao_s, ga_s, g_attn_ref, y_s, 0)
    _out_attn_half(w_out_ref, g_fin_ref, o_ref, y_s, final)


def _prompt_kernel(sinks_ref, x_ref, *refs, layer):
    weights, (o_ref, k_out_ref, v_out_ref), scratch = refs[:11], refs[11:14], refs[14:]
    k_ext, vt_ext = scratch[2], scratch[3]
    t = pl.program_id(1)

    @pl.when(t == 0)
    def _():
        k_ext[0:WINDOW, :] = jnp.zeros((WINDOW, D_KV), BF16)
        vt_ext[:, 0:WINDOW] = jnp.zeros((D_KV, WINDOW), BF16)

    for sub in range(TILES_PER_STEP):
        rows = slice(sub * TM, (sub + 1) * TM)
        last = sub == TILES_PER_STEP - 1
        _prompt_tile(sinks_ref, x_ref.at[rows, :], *weights, o_ref.at[rows, :],
                     (k_out_ref, v_out_ref) if last else None, *scratch,
                     layer=layer, first=(t == 0) if sub == 0 else None)


def _sample_kernel(sinks_ref, x_ref, ck_ref, cv_ref, w_in_ref, w_out_ref, wsp_ref, bsp_ref, bias_ref,
                   g_in_ref, g_attn_ref, g_gmlp_ref, ln_g_ref, ln_b_ref, g_fin_ref,
                   o_ref, nk_ref, nv_ref, vg_ref,
                   h_s, q_s, ga_s, gb_s, u_s, vlo_s, vhi_s, ao_s, y_s,
                   *, layer, t_len):
    final = layer == DEPTH - 1
    n_past = ck_ref.shape[1]
    q_cols = GQA_GROUP * t_len
    _rms_to_bf16(x_ref, g_in_ref, h_s)
    k, v = _proj_kvq(h_s, w_in_ref, q_s)

    nk_ref[:, 0:n_past - t_len, :] = ck_ref[:, t_len:, :]
    nv_ref[:, 0:n_past - t_len, :] = cv_ref[:, t_len:, :]
    nk_ref[:, n_past - t_len:, :] = k.reshape(NB, t_len, D_KV)
    nv_ref[:, n_past - t_len:, :] = v.reshape(NB, t_len, D_KV)

    kb = k.astype(BF16)
    vt_new = v.T.astype(BF16)
    vt_past = [cv_ref[b].T.astype(BF16) for b in range(NB)]

    def gmlp_mix():
        for j in range(N_PAIRS):
            w = _masked_spatial(wsp_ref, j, t_len)
            cols = slice(j * LANES, (j + 1) * LANES)
            rhs = jnp.concatenate(
                [jnp.concatenate([vlo_s[b * t_len:(b + 1) * t_len, cols],
                                  vhi_s[b * t_len:(b + 1) * t_len, cols]], axis=0)
                 for b in range(NB)], axis=1)
            mix = jnp.dot(w, rhs, preferred_element_type=F32)
            for b in range(NB):
                rws = slice(b * t_len, (b + 1) * t_len)
                u_s[rws, cols] = u_s[rws, cols] * (mix[:, b * LANES:(b + 1) * LANES] + bsp_ref[j])

    def scores(i):
        b, kvh = divmod(i, N_KV_HEADS)
        r0 = b * t_len
        c0 = kvh * GQA_GROUP * HEAD_DIM
        hc = slice(kvh * HEAD_DIM, (kvh + 1) * HEAD_DIM)
        q_rows = jnp.concatenate(
            [q_s[r0:r0 + t_len, c0 + g * HEAD_DIM:c0 + (g + 1) * HEAD_DIM]
             for g in range(GQA_GROUP)], axis=0)
        keys = jnp.concatenate([ck_ref[b, :, hc].astype(BF16), kb[r0:r0 + t_len, hc]], axis=0)
        return _scores_t(keys, q_rows)

    def finish(i, s, after_row):
        b, kvh = divmod(i, N_KV_HEADS)
        r0 = b * t_len
        c0 = kvh * GQA_GROUP * HEAD_DIM
        hc = slice(kvh * HEAD_DIM, (kvh + 1) * HEAD_DIM)
        vals_t = jnp.concatenate([vt_past[b][hc, :], vt_new[hc, r0:r0 + t_len]], axis=1)
        o = _softmax_pv_t(s, vals_t, bias_ref[kvh],
                          _sink_row(sinks_ref, layer, kvh, t_len), after_row).T
        for g in range(GQA_GROUP):
            cg = c0 + g * HEAD_DIM
            ao_s[r0:r0 + t_len, cg:cg + HEAD_DIM] = o[g * t_len:(g + 1) * t_len, :]

    chunks = _act_chunks(h_s, w_in_ref, vg_ref, u_s, gb_s, ga_s)
    per = (NB * N_KV_HEADS) // len(chunks)
    chunks[0]()
    s_next = [scores(i) for i in range(per)]
    for j in range(len(chunks)):
        s_cur = s_next
        if j + 1 < len(chunks):
            s_next = [scores((j + 1) * per + i) for i in range(per)]
            after = [s[0:1, :] for s in s_next]
            chunks[j + 1]()
        else:
            _out_gmlp_half(x_ref, w_out_ref, o_ref, y_s, 0, PROJ_COLS)
            after = [o_ref[0:1, 0:q_cols]] * per
        for i in range(per):
            finish(j * per + i, s_cur[i], after[i])
        if j == 0:
            _layernorm_split(vg_ref, ln_g_ref, ln_b_ref, vlo_s, vhi_s)
        if j == 2:
            gmlp_mix()
        if j == 4:
            _gated_norm(u_s, gb_s, g_gmlp_ref, y_s, D_ATTN)
    _out_gmlp_half(x_ref, w_out_ref, o_ref, y_s, PROJ_COLS, D_MODEL)

    _gated_norm(ao_s, ga_s, g_attn_ref, y_s, 0)
    _out_attn_half(w_out_ref, g_fin_ref, o_ref, y_s, final)


def _resident(shape):
    nd = len(shape)
    return pl.BlockSpec(shape, lambda *_: (0,) * nd, pipeline_mode=pl.Buffered(1))


def _layer_slice(arr, layer):
    tail = arr.shape[1:]
    return pl.BlockSpec((None,) + tail, lambda *_: (layer,) + (0,) * len(tail),
                        pipeline_mode=pl.Buffered(1))


def _weight_specs(layer, w_in, w_out, wsp, bsp, bias, vecs):
    return ([_layer_slice(w_in, layer), _layer_slice(w_out, layer), _layer_slice(wsp, layer),
             _layer_slice(bsp, layer), _resident(bias.shape)]
            + [_layer_slice(vec, layer) for vec in vecs[:-1]] + [_resident(vecs[-1].shape)])


def _prompt_layer(layer, x, sinks, w_in, w_out, wsp, bsp, bias, vecs):
    batch, seq, _ = x.shape
    kv_shape = jax.ShapeDtypeStruct((batch, WINDOW, D_KV), F32)
    kv_spec = pl.BlockSpec((None, WINDOW, D_KV), lambda b, t: (b, 0, 0))
    x_spec = pl.BlockSpec((None, TILES_PER_STEP * TM, D_MODEL), lambda b, t: (b, t, 0))
    final = layer == DEPTH - 1
    return pl.pallas_call(
        functools.partial(_prompt_kernel, layer=layer),
        grid=(batch, seq // (TILES_PER_STEP * TM)),
        in_specs=[pl.BlockSpec(memory_space=pltpu.SMEM), x_spec]
        + _weight_specs(layer, w_in, w_out, wsp, bsp, bias, vecs),
        out_specs=[x_spec, kv_spec, kv_spec],
        out_shape=[jax.ShapeDtypeStruct(x.shape, F32), kv_shape, kv_shape],
        scratch_shapes=[
            pltpu.VMEM((TM, D_MODEL), BF16),
            pltpu.VMEM((TM, D_ATTN), BF16),
            pltpu.VMEM((WINDOW + TM, D_KV), BF16),
            pltpu.VMEM((D_KV, WINDOW + TM), BF16),
            pltpu.VMEM((TM, D_ATTN), F32),
            pltpu.VMEM((TM, D_GMLP), F32),
            pltpu.VMEM((TM, D_GMLP), F32),
            pltpu.VMEM((TM, D_GMLP), F32),
            pltpu.VMEM((TM, D_GMLP), BF16),
            pltpu.VMEM((TM, D_GMLP), BF16),
            pltpu.VMEM((TM, D_ATTN), F32),
            pltpu.VMEM((TM, D_MODEL), BF16),
        ],
        compiler_params=pltpu.CompilerParams(
            dimension_semantics=("arbitrary", "arbitrary"), vmem_limit_bytes=VMEM_LIMIT),
        name="prompt_layer_final" if final else "prompt_layer",
    )(sinks, x, w_in, w_out, wsp, bsp, bias, *vecs)


def _sample_layer(layer, x, ck, cv, sinks, w_in, w_out, wsp, bsp, bias, vecs, t_len):
    rows = x.shape[0]
    _, n_streams, n_past, _ = ck.shape
    step_rows = NB * t_len
    x_spec = pl.BlockSpec((step_rows, D_MODEL), lambda i: (i, 0))
    cin_spec = pl.BlockSpec((None, NB, n_past, D_KV), lambda i: (layer, i, 0, 0))
    cout_spec = pl.BlockSpec((NB, n_past, D_KV), lambda i: (i, 0, 0))
    vg_spec = pl.BlockSpec((step_rows, D_GMLP), lambda i: (i, 0))
    cache_shape = jax.ShapeDtypeStruct((n_streams, n_past, D_KV), F32)
    final = layer == DEPTH - 1
    return pl.pallas_call(
        functools.partial(_sample_kernel, layer=layer, t_len=t_len),
        grid=(n_streams // NB,),
        in_specs=[pl.BlockSpec(memory_space=pltpu.SMEM), x_spec, cin_spec, cin_spec]
        + _weight_specs(layer, w_in, w_out, wsp, bsp, bias, vecs),
        out_specs=[x_spec, cout_spec, cout_spec, vg_spec],
        out_shape=[jax.ShapeDtypeStruct(x.shape, F32), cache_shape, cache_shape,
                   jax.ShapeDtypeStruct((rows, D_GMLP), F32)],
        scratch_shapes=[
            pltpu.VMEM((step_rows, D_MODEL), BF16),
            pltpu.VMEM((step_rows, D_ATTN), BF16),
            pltpu.VMEM((step_rows, D_ATTN), F32),
            pltpu.VMEM((step_rows, D_GMLP), F32),
            pltpu.VMEM((step_rows, D_GMLP), F32),
            pltpu.VMEM((step_rows, D_GMLP), BF16),
            pltpu.VMEM((step_rows, D_GMLP), BF16),
            pltpu.VMEM((step_rows, D_ATTN), F32),
            pltpu.VMEM((step_rows, D_MODEL), BF16),
        ],
        compiler_params=pltpu.CompilerParams(
            dimension_semantics=("arbitrary",), vmem_limit_bytes=VMEM_LIMIT),
        name="sample_layer_final" if final else "sample_layer",
    )(sinks, x, ck, cv, w_in, w_out, wsp, bsp, bias, *vecs)


def kernel(x_prompt, x_sample, cache_k, cache_v, w_in, w_out, norm_in, rel_bias, sinks,
           norm_attn, norm_gmlp, ln_v_g, ln_v_b, w_spatial, b_spatial, norm_final):
    bp, seq, _ = x_prompt.shape
    bs, t_len, _ = x_sample.shape
    n_past = cache_k.shape[2]
    assert seq % (TILES_PER_STEP * TM) == 0 and TM % PAIR_ROWS == 0 and TM % GMLP_CHUNK == 0 and TM >= WINDOW
    assert bs % NB == 0 and t_len <= GMLP_CHUNK and n_past >= t_len
    assert TM % ROW_BLOCK == 0 and (NB * t_len) % ROW_BLOCK == 0
    assert (TM // PAIR_ROWS) * N_KV_HEADS == 4 * (D_GMLP // PROJ_COLS)

    bias_p = _prompt_bias_table(rel_bias)
    bias_s = _sample_bias_table(rel_bias, t_len, n_past)
    w_in_b = w_in.astype(BF16)
    w_out_b = w_out.astype(BF16)
    vecs = (norm_in[:, None], norm_attn[:, None], norm_gmlp[:, None],
            ln_v_g[:, None], ln_v_b[:, None], norm_final[None])
    wsp_p, bsp_p = _pair_spatial(w_spatial, b_spatial, GMLP_CHUNK)
    wsp_s, bsp_s = _pair_spatial(w_spatial, b_spatial, t_len)
    ck = cache_k.reshape(DEPTH, bs, n_past, D_KV)
    cv = cache_v.reshape(DEPTH, bs, n_past, D_KV)

    xp = x_prompt
    xs = x_sample.reshape(bs * t_len, D_MODEL)
    kp, vp, ks, vs, vgs = [], [], [], [], []
    for l in range(DEPTH):
        xp, k_l, v_l = _prompt_layer(l, xp, sinks, w_in_b, w_out_b, wsp_p, bsp_p, bias_p, vecs)
        kp.append(k_l)
        vp.append(v_l)
        xs, nk, nv, vg = _sample_layer(l, xs, ck, cv, sinks, w_in_b, w_out_b, wsp_s, bsp_s,
                                       bias_s, vecs, t_len)
        ks.append(nk)
        vs.append(nv)
        vgs.append(vg)

    def heads(rows, n):
        return jnp.stack(rows).reshape(DEPTH, n, -1, N_KV_HEADS, HEAD_DIM)

    return (xp, xs.reshape(bs, t_len, D_MODEL),
            heads(kp, bp), heads(vp, bp), heads(ks, bs), heads(vs, bs),
            jnp.stack(vgs).reshape(DEPTH, bs, t_len, N_GROUPS_B, HEAD_DIM))
```

```python
import functools
import math

import jax
import jax.numpy as jnp
from jax import lax
from jax.experimental import pallas as pl
from jax.experimental.pallas import tpu as pltpu

D_MODEL = 2048
DEPTH = 4
CHUNK = 64
HEAD_DIM = 64
D_ATTN = 1024
D_GMLP = 1024
N_HEADS = 16
N_KV_HEADS = 4
GQA_GROUP = 4
D_KV = 256
WINDOW = 128
N_BUCKETS = 32
MAX_DISTANCE = 128
GMLP_CHUNK = 128
N_GROUPS_B = 16
D_PROJ = 5632
EPS = 1e-6
NEG_INF = -1e30

O_Q, O_K, O_V, O_GA, O_U, O_VG, O_GB = 0, 1024, 1280, 1536, 2560, 3584, 4608

LANES = 128
BF16_ROWS = 16
N_PAIRS = N_GROUPS_B // 2
PAIR_ROWS = 2 * CHUNK
PAIR_KEYS = 4 * CHUNK
TM = 256
TILES_PER_STEP = 2
NB = 4
PROJ_COLS = 512
ROW_BLOCK = 32
VMEM_LIMIT = 58 * 1024 * 1024

F32 = jnp.float32
BF16 = jnp.bfloat16


def _rel_bucket(rel):
    nb = N_BUCKETS // 2
    max_exact = nb // 2
    base = jnp.where(rel > 0, nb, 0)
    n = jnp.abs(rel)
    nf = jnp.maximum(n, 1).astype(F32)
    large = max_exact + (jnp.log(nf / max_exact) / math.log(MAX_DISTANCE / max_exact)
                         * (nb - max_exact)).astype(jnp.int32)
    large = jnp.minimum(large, nb - 1)
    return base + jnp.where(n < max_exact, n, large)


def _band_bias(rel_bias, n_q, n_k, n_past):
    n = n_q + n_k - 1
    rel = jnp.arange(n) - (n_q - 1) - n_past
    v = rel_bias[_rel_bucket(rel)].astype(F32).T
    w = jnp.concatenate([v, jnp.zeros((N_HEADS, 1), F32)], axis=1)
    band = jnp.tile(w, (1, n_q))[:, :n_q * n].reshape(N_HEADS, n_q, n)
    return band[:, :, n_q - 1:]


def _prompt_bias_table(rel_bias):
    b = _band_bias(rel_bias, PAIR_ROWS, PAIR_KEYS, WINDOW)
    b = b.reshape(N_KV_HEADS, GQA_GROUP, PAIR_ROWS, PAIR_KEYS).transpose(0, 3, 1, 2)
    r = jnp.arange(PAIR_ROWS)[None, :]
    j = jnp.arange(PAIR_KEYS)[:, None]
    qc = r // CHUNK
    kc = j // CHUNK - 2
    in_window = (kc <= qc) & (kc >= qc - 2)
    t = jnp.where(in_window[None, :, None, :], b, NEG_INF)
    return t.reshape(N_KV_HEADS, PAIR_KEYS, GQA_GROUP * PAIR_ROWS)


def _sample_bias_table(rel_bias, n_q, n_past):
    b = _band_bias(rel_bias, n_q, n_past + n_q, n_past)
    return jnp.swapaxes(b.reshape(N_KV_HEADS, GQA_GROUP * n_q, n_past + n_q), 1, 2)


def _pair_spatial(w_s, b_s, rows):
    depth = w_s.shape[0]
    w = w_s[:, :, :rows, :rows].reshape(depth, N_PAIRS, 2, rows, rows)
    w = jnp.transpose(w, (0, 1, 3, 2, 4)).reshape(depth, N_PAIRS, rows, 2 * rows)
    b = b_s[:, :, :rows].reshape(depth, N_PAIRS, 2, rows)
    b = jnp.repeat(jnp.transpose(b, (0, 1, 3, 2)), LANES // 2, axis=3)
    return w, b


def _silu(x):
    return x / (1.0 + jnp.exp(-x))


def _gelu(x):
    return 0.5 * x * (1.0 + lax.erf(x * math.sqrt(0.5)))


def _rms_to_bf16(x_ref, g_ref, h_s):
    for r in range(0, x_ref.shape[0], ROW_BLOCK):
        x = x_ref[r:r + ROW_BLOCK, :]
        inv = lax.rsqrt(jnp.mean(x * x, axis=-1, keepdims=True) + EPS)
        h_s[r:r + ROW_BLOCK, :] = (x * inv * g_ref[...]).astype(BF16)


def _proj(h_s, w_in_ref, off, width):
    return jnp.dot(h_s[...], w_in_ref[:, off:off + width], preferred_element_type=F32)


def _proj_kvq(h_s, w_in_ref, q_s, q_transposed=False):
    kv = _proj(h_s, w_in_ref, O_K, 2 * D_KV)
    for c in range(0, D_ATTN, PROJ_COLS):
        q = _proj(h_s, w_in_ref, O_Q + c, PROJ_COLS) * (HEAD_DIM ** -0.5)
        if q_transposed:
            q_s[c:c + PROJ_COLS, :] = q.T.astype(BF16)
        else:
            q_s[:, c:c + PROJ_COLS] = q.astype(BF16)
    return kv[:, :D_KV], kv[:, D_KV:]


def _act_chunks(h_s, w_in_ref, vgf_s, u_s, gb_s, ga_s):
    def make(dst, off, act, c):
        def run():
            dst[:, c:c + PROJ_COLS] = act(_proj(h_s, w_in_ref, off + c, PROJ_COLS))
        return run
    runs = [make(dst, off, act, c)
            for dst, off, act in ((ga_s, O_GA, _silu), (vgf_s, O_VG, _gelu), (u_s, O_U, _gelu),
                                  (gb_s, O_GB, _silu))
            for c in range(0, D_GMLP, PROJ_COLS)]
    n = D_GMLP // PROJ_COLS
    return runs[:1] + runs[n:] + runs[1:n]


def _layernorm_split(vgf_s, ln_g_ref, ln_b_ref, vlo_s, vhi_s):
    low = (lax.broadcasted_iota(jnp.int32, (ROW_BLOCK, D_GMLP), 1) % LANES) < (LANES // 2)
    for r in range(0, vgf_s.shape[0], ROW_BLOCK):
        g = vgf_s[r:r + ROW_BLOCK, :]
        gc = g - jnp.mean(g, axis=-1, keepdims=True)
        y = gc * lax.rsqrt(jnp.mean(gc * gc, axis=-1, keepdims=True) + EPS)
        y = y * ln_g_ref[...] + ln_b_ref[...]
        vgf_s[r:r + ROW_BLOCK, :] = y
        vlo_s[r:r + ROW_BLOCK, :] = jnp.where(low, y, 0.0).astype(BF16)
        vhi_s[r:r + ROW_BLOCK, :] = jnp.where(low, 0.0, y).astype(BF16)


def _scores_t(keys, q_rows):
    return lax.dot_general(keys, q_rows, (((1,), (1,)), ((), ())), preferred_element_type=F32)


def _zero_after(row):
    u = lax.bitcast_convert_type(row, jnp.uint32)
    u = lax.shift_right_logical(lax.shift_right_logical(u, jnp.uint32(16)), jnp.uint32(16))
    return lax.bitcast_convert_type(u, F32)


def _softmax_pv_t(s, vals_t, bias_t, sink_row, after_row, mask_past=None):
    s = s + bias_t
    if mask_past is not None:
        s = jnp.concatenate([jnp.where(mask_past, NEG_INF, s[:WINDOW]), s[WINDOW:]], axis=0)
    m = jnp.maximum(jnp.max(s, axis=0, keepdims=True), sink_row)
    e = jnp.exp(s - m)
    l = jnp.sum(e, axis=0, keepdims=True) + jnp.exp(sink_row - m)
    tail = e.shape[0] - BF16_ROWS
    p = jnp.concatenate([e[:tail].astype(BF16),
                         (e[tail:] + _zero_after(after_row)).astype(BF16)], axis=0)
    o = jnp.dot(vals_t, p, preferred_element_type=F32)
    return o * (1.0 / l)


def _sink_row(sinks_ref, layer, kvh, n):
    shape = (1, GQA_GROUP * n)
    g = lax.broadcasted_iota(jnp.int32, shape, 1) // n
    out = jnp.full(shape, sinks_ref[layer, kvh * GQA_GROUP], F32)
    for i in range(1, GQA_GROUP):
        out = jnp.where(g == i, sinks_ref[layer, kvh * GQA_GROUP + i], out)
    return out


def _masked_spatial(wsp_ref, j, rows):
    w = wsp_ref[j]
    i = lax.broadcasted_iota(jnp.int32, w.shape, 0)
    c = lax.broadcasted_iota(jnp.int32, w.shape, 1) % rows
    return jnp.where(c <= i, w, 0.0).astype(BF16)


def _gated_norm(src_s, gate_s, g_ref, y_s, col0):
    rows, width = src_s.shape
    for r in range(0, rows, ROW_BLOCK):
        a = src_s[r:r + ROW_BLOCK, :]
        inv = lax.rsqrt(jnp.mean(a * a, axis=-1, keepdims=True) + EPS)
        y_s[r:r + ROW_BLOCK, col0:col0 + width] = (
            a * inv * g_ref[...] * gate_s[r:r + ROW_BLOCK, :]).astype(BF16)


def _out_gmlp_half(x_ref, w_out_ref, o_ref, y_s, col_lo=0, col_hi=D_MODEL):
    for c in range(col_lo, col_hi, PROJ_COLS):
        o_ref[:, c:c + PROJ_COLS] = x_ref[:, c:c + PROJ_COLS] + jnp.dot(
            y_s[:, D_ATTN:], w_out_ref[D_ATTN:, c:c + PROJ_COLS], preferred_element_type=F32)


def _out_attn_half(w_out_ref, g_fin_ref, o_ref, y_s, final):
    rows = o_ref.shape[0]
    for c in range(0, D_MODEL, PROJ_COLS):
        o_ref[:, c:c + PROJ_COLS] = o_ref[:, c:c + PROJ_COLS] + jnp.dot(
            y_s[:, :D_ATTN], w_out_ref[:D_ATTN, c:c + PROJ_COLS], preferred_element_type=F32)
    if final:
        for r in range(0, rows, ROW_BLOCK):
            xn = o_ref[r:r + ROW_BLOCK, :]
            inv = lax.rsqrt(jnp.mean(xn * xn, axis=-1, keepdims=True) + EPS)
            o_ref[r:r + ROW_BLOCK, :] = xn * inv * g_fin_ref[...]


def _prompt_tile(sinks_ref, x_ref, w_in_ref, w_out_ref, wsp_ref, bsp_ref, bias_ref,
                 g_in_ref, g_attn_ref, g_gmlp_ref, ln_g_ref, ln_b_ref, g_fin_ref,
                 o_ref, kv_out_refs,
                 h_s, q_s, k_ext, vt_ext, ga_s, gb_s, u_s, vgf_s, vlo_s, vhi_s, ao_s, y_s,
                 *, layer, first):
    final = layer == DEPTH - 1
    _rms_to_bf16(x_ref, g_in_ref, h_s)
    k, v = _proj_kvq(h_s, w_in_ref, q_s, q_transposed=True)
    k_ext[WINDOW:, :] = k.astype(BF16)
    vt_ext[:, WINDOW:] = v.T.astype(BF16)
    if kv_out_refs is not None:
        kv_out_refs[0][...] = k[TM - WINDOW:, :]
        kv_out_refs[1][...] = v[TM - WINDOW:, :]

    def gmlp_mix():
        for j in range(N_PAIRS):
            w = _masked_spatial(wsp_ref, j, GMLP_CHUNK)
            cols = slice(j * LANES, (j + 1) * LANES)
            for c in range(TM // GMLP_CHUNK):
                rws = slice(c * GMLP_CHUNK, (c + 1) * GMLP_CHUNK)
                rhs = jnp.concatenate([vlo_s[rws, cols], vhi_s[rws, cols]], axis=0)
                mix = jnp.dot(w, rhs, preferred_element_type=F32) + bsp_ref[j]
                u_s[rws, cols] = u_s[rws, cols] * mix

    def store_block(p, kvh, o_t):
        o_gd = jnp.concatenate(
            [o_t[:, g * PAIR_ROWS:(g + 1) * PAIR_ROWS] for g in range(GQA_GROUP)], axis=0)
        ao_s[p * PAIR_ROWS:(p + 1) * PAIR_ROWS, kvh * D_KV:(kvh + 1) * D_KV] = o_gd.T

    def scores(i):
        p, kvh = divmod(i, N_KV_HEADS)
        r0 = p * PAIR_ROWS
        c0 = kvh * GQA_GROUP * HEAD_DIM
        q_t = jnp.concatenate(
            [q_s[c0 + g * HEAD_DIM:c0 + (g + 1) * HEAD_DIM, r0:r0 + PAIR_ROWS]
             for g in range(GQA_GROUP)], axis=1)
        zeros = jnp.zeros_like(q_t)
        q_t = jnp.concatenate([q_t, zeros] if kvh % 2 == 0 else [zeros, q_t], axis=0)
        lanes = slice((kvh // 2) * LANES, (kvh // 2 + 1) * LANES)
        return jnp.dot(k_ext[r0:r0 + PAIR_KEYS, lanes], q_t, preferred_element_type=F32)

    chunks = _act_chunks(h_s, w_in_ref, vgf_s, u_s, gb_s, ga_s)
    n_blocks = (TM // PAIR_ROWS) * N_KV_HEADS
    chunks[0]()
    s_next = scores(0)
    for i in range(n_blocks):
        p, kvh = divmod(i, N_KV_HEADS)
        r0 = p * PAIR_ROWS
        hc = slice(kvh * HEAD_DIM, (kvh + 1) * HEAD_DIM)
        s = s_next
        if i + 1 < n_blocks:
            s_next = scores(i + 1)
            after_row = s_next[0:1, :]
            chunks[i + 1]()
        else:
            _out_gmlp_half(x_ref, w_out_ref, o_ref, y_s, 0, 2 * PROJ_COLS)
            after_row = o_ref[0:1, 0:PROJ_COLS]
        store_block(p, kvh, _softmax_pv_t(
            s, vt_ext[hc, r0:r0 + PAIR_KEYS], bias_ref[kvh],
            _sink_row(sinks_ref, layer, kvh, PAIR_ROWS), after_row,
            mask_past=first if p == 0 else None))
        if i == 1:
            _layernorm_split(vgf_s, ln_g_ref, ln_b_ref, vlo_s, vhi_s)
        if i == 3:
            gmlp_mix()
        if i == 5:
            _gated_norm(u_s, gb_s, g_gmlp_ref, y_s, D_ATTN)
    _out_gmlp_half(x_ref, w_out_ref, o_ref, y_s, 2 * PROJ_COLS, D_MODEL)

    k_ext[0:WINDOW, :] = k_ext[TM:TM + WINDOW, :]
    vt_ext[:, 0:WINDOW] = vt_ext[:, TM:TM + WINDOW]

    _gated_norm(ao_s, ga_s, g_attn_ref, y_s, 0)
    _out_attn_half(w_out_ref, g_fin_ref, o_ref, y_s, final)


def _prompt_kernel(sinks_ref, x_ref, *refs, layer):
    weights, (o_ref, k_out_ref, v_out_ref), scratch = refs[:11], refs[11:14], refs[14:]
    k_ext, vt_ext = scratch[2], scratch[3]
    t = pl.program_id(1)

    @pl.when(t == 0)
    def _():
        k_ext[0:WINDOW, :] = jnp.zeros((WINDOW, D_KV), BF16)
        vt_ext[:, 0:WINDOW] = jnp.zeros((D_KV, WINDOW), BF16)

    for sub in range(TILES_PER_STEP):
        rows = slice(sub * TM, (sub + 1) * TM)
        last = sub == TILES_PER_STEP - 1
        _prompt_tile(sinks_ref, x_ref.at[rows, :], *weights, o_ref.at[rows, :],
                     (k_out_ref, v_out_ref) if last else None, *scratch,
                     layer=layer, first=(t == 0) if sub == 0 else None)


def _sample_kernel(sinks_ref, x_ref, ck_ref, cv_ref, w_in_ref, w_out_ref, wsp_ref, bsp_ref, bias_ref,
                   g_in_ref, g_attn_ref, g_gmlp_ref, ln_g_ref, ln_b_ref, g_fin_ref,
                   o_ref, nk_ref, nv_ref, vg_ref,
                   h_s, q_s, ga_s, gb_s, u_s, vlo_s, vhi_s, ao_s, y_s,
                   *, layer, t_len):
    final = layer == DEPTH - 1
    n_past = ck_ref.shape[1]
    q_cols = GQA_GROUP * t_len
    _rms_to_bf16(x_ref, g_in_ref, h_s)
    k, v = _proj_kvq(h_s, w_in_ref, q_s)

    nk_ref[:, 0:n_past - t_len, :] = ck_ref[:, t_len:, :]
    nv_ref[:, 0:n_past - t_len, :] = cv_ref[:, t_len:, :]
    nk_ref[:, n_past - t_len:, :] = k.reshape(NB, t_len, D_KV)
    nv_ref[:, n_past - t_len:, :] = v.reshape(NB, t_len, D_KV)

    kb = k.astype(BF16)
    vt_new = v.T.astype(BF16)
    vt_past = [cv_ref[b].T.astype(BF16) for b in range(NB)]

    def gmlp_mix():
        for j in range(N_PAIRS):
            w = _masked_spatial(wsp_ref, j, t_len)
            cols = slice(j * LANES, (j + 1) * LANES)
            rhs = jnp.concatenate(
                [jnp.concatenate([vlo_s[b * t_len:(b + 1) * t_len, cols],
                                  vhi_s[b * t_len:(b + 1) * t_len, cols]], axis=0)
                 for b in range(NB)], axis=1)
            mix = jnp.dot(w, rhs, preferred_element_type=F32)
            for b in range(NB):
                rws = slice(b * t_len, (b + 1) * t_len)
                u_s[rws, cols] = u_s[rws, cols] * (mix[:, b * LANES:(b + 1) * LANES] + bsp_ref[j])

    def scores(i):
        b, kvh = divmod(i, N_KV_HEADS)
        r0 = b * t_len
        c0 = kvh * GQA_GROUP * HEAD_DIM
        hc = slice(kvh * HEAD_DIM, (kvh + 1) * HEAD_DIM)
        q_rows = jnp.concatenate(
            [q_s[r0:r0 + t_len, c0 + g * HEAD_DIM:c0 + (g + 1) * HEAD_DIM]
             for g in range(GQA_GROUP)], axis=0)
        keys = jnp.concatenate([ck_ref[b, :, hc].astype(BF16), kb[r0:r0 + t_len, hc]], axis=0)
        return _scores_t(keys, q_rows)

    def finish(i, s, after_row):
        b, kvh = divmod(i, N_KV_HEADS)
        r0 = b * t_len
        c0 = kvh * GQA_GROUP * HEAD_DIM
        hc = slice(kvh * HEAD_DIM, (kvh + 1) * HEAD_DIM)
        vals_t = jnp.concatenate([vt_past[b][hc, :], vt_new[hc, r0:r0 + t_len]], axis=1)
        o = _softmax_pv_t(s, vals_t, bias_ref[kvh],
                          _sink_row(sinks_ref, layer, kvh, t_len), after_row).T
        for g in range(GQA_GROUP):
            cg = c0 + g * HEAD_DIM
            ao_s[r0:r0 + t_len, cg:cg + HEAD_DIM] = o[g * t_len:(g + 1) * t_len, :]

    chunks = _act_chunks(h_s, w_in_ref, vg_ref, u_s, gb_s, ga_s)
    per = (NB * N_KV_HEADS) // len(chunks)
    chunks[0]()
    s_next = [scores(i) for i in range(per)]
    for j in range(len(chunks)):
        s_cur = s_next
        if j + 1 < len(chunks):
            s_next = [scores((j + 1) * per + i) for i in range(per)]
            after = [s[0:1, :] for s in s_next]
            chunks[j + 1]()
        else:
            _out_gmlp_half(x_ref, w_out_ref, o_ref, y_s, 0, PROJ_COLS)
            after = [o_ref[0:1, 0:q_cols]] * per
        for i in range(per):
            finish(j * per + i, s_cur[i], after[i])
        if j == 1:
            _layernorm_split(vg_ref, ln_g_ref, ln_b_ref, vlo_s, vhi_s)
        if j == 3:
            gmlp_mix()
        if j == 5:
            _gated_norm(u_s, gb_s, g_gmlp_ref, y_s, D_ATTN)
    _out_gmlp_half(x_ref, w_out_ref, o_ref, y_s, PROJ_COLS, D_MODEL)

    _gated_norm(ao_s, ga_s, g_attn_ref, y_s, 0)
    _out_attn_half(w_out_ref, g_fin_ref, o_ref, y_s, final)


def _resident(shape):
    nd = len(shape)
    return pl.BlockSpec(shape, lambda *_: (0,) * nd, pipeline_mode=pl.Buffered(1))


def _layer_slice(arr, layer):
    tail = arr.shape[1:]
    return pl.BlockSpec((None,) + tail, lambda *_: (layer,) + (0,) * len(tail),
                        pipeline_mode=pl.Buffered(1))


def _weight_specs(layer, w_in, w_out, wsp, bsp, bias, vecs):
    return ([_layer_slice(w_in, layer), _layer_slice(w_out, layer), _layer_slice(wsp, layer),
             _layer_slice(bsp, layer), _resident(bias.shape)]
            + [_layer_slice(vec, layer) for vec in vecs[:-1]] + [_resident(vecs[-1].shape)])


def _prompt_layer(layer, x, sinks, w_in, w_out, wsp, bsp, bias, vecs):
    batch, seq, _ = x.shape
    kv_shape = jax.ShapeDtypeStruct((batch, WINDOW, D_KV), F32)
    kv_spec = pl.BlockSpec((None, WINDOW, D_KV), lambda b, t: (b, 0, 0))
    x_spec = pl.BlockSpec((None, TILES_PER_STEP * TM, D_MODEL), lambda b, t: (b, t, 0))
    final = layer == DEPTH - 1
    return pl.pallas_call(
        functools.partial(_prompt_kernel, layer=layer),
        grid=(batch, seq // (TILES_PER_STEP * TM)),
        in_specs=[pl.BlockSpec(memory_space=pltpu.SMEM), x_spec]
        + _weight_specs(layer, w_in, w_out, wsp, bsp, bias, vecs),
        out_specs=[x_spec, kv_spec, kv_spec],
        out_shape=[jax.ShapeDtypeStruct(x.shape, F32), kv_shape, kv_shape],
        scratch_shapes=[
            pltpu.VMEM((TM, D_MODEL), BF16),
            pltpu.VMEM((D_ATTN, TM), BF16),
            pltpu.VMEM((WINDOW + TM, D_KV), BF16),
            pltpu.VMEM((D_KV, WINDOW + TM), BF16),
            pltpu.VMEM((TM, D_ATTN), F32),
            pltpu.VMEM((TM, D_GMLP), F32),
            pltpu.VMEM((TM, D_GMLP), F32),
            pltpu.VMEM((TM, D_GMLP), F32),
            pltpu.VMEM((TM, D_GMLP), BF16),
            pltpu.VMEM((TM, D_GMLP), BF16),
            pltpu.VMEM((TM, D_ATTN), F32),
            pltpu.VMEM((TM, D_MODEL), BF16),
        ],
        compiler_params=pltpu.CompilerParams(
            dimension_semantics=("arbitrary", "arbitrary"), vmem_limit_bytes=VMEM_LIMIT),
        name="prompt_layer_final" if final else "prompt_layer",
    )(sinks, x, w_in, w_out, wsp, bsp, bias, *vecs)


def _sample_layer(layer, x, ck, cv, sinks, w_in, w_out, wsp, bsp, bias, vecs, t_len):
    rows = x.shape[0]
    _, n_streams, n_past, _ = ck.shape
    step_rows = NB * t_len
    x_spec = pl.BlockSpec((step_rows, D_MODEL), lambda i: (i, 0))
    cin_spec = pl.BlockSpec((None, NB, n_past, D_KV), lambda i: (layer, i, 0, 0))
    cout_spec = pl.BlockSpec((NB, n_past, D_KV), lambda i: (i, 0, 0))
    vg_spec = pl.BlockSpec((step_rows, D_GMLP), lambda i: (i, 0))
    cache_shape = jax.ShapeDtypeStruct((n_streams, n_past, D_KV), F32)
    final = layer == DEPTH - 1
    return pl.pallas_call(
        functools.partial(_sample_kernel, layer=layer, t_len=t_len),
        grid=(n_streams // NB,),
        in_specs=[pl.BlockSpec(memory_space=pltpu.SMEM), x_spec, cin_spec, cin_spec]
        + _weight_specs(layer, w_in, w_out, wsp, bsp, bias, vecs),
        out_specs=[x_spec, cout_spec, cout_spec, vg_spec],
        out_shape=[jax.ShapeDtypeStruct(x.shape, F32), cache_shape, cache_shape,
                   jax.ShapeDtypeStruct((rows, D_GMLP), F32)],
        scratch_shapes=[
            pltpu.VMEM((step_rows, D_MODEL), BF16),
            pltpu.VMEM((step_rows, D_ATTN), BF16),
            pltpu.VMEM((step_rows, D_ATTN), F32),
            pltpu.VMEM((step_rows, D_GMLP), F32),
            pltpu.VMEM((step_rows, D_GMLP), F32),
            pltpu.VMEM((step_rows, D_GMLP), BF16),
            pltpu.VMEM((step_rows, D_GMLP), BF16),
            pltpu.VMEM((step_rows, D_ATTN), F32),
            pltpu.VMEM((step_rows, D_MODEL), BF16),
        ],
        compiler_params=pltpu.CompilerParams(
            dimension_semantics=("arbitrary",), vmem_limit_bytes=VMEM_LIMIT),
        name="sample_layer_final" if final else "sample_layer",
    )(sinks, x, ck, cv, w_in, w_out, wsp, bsp, bias, *vecs)


def kernel(x_prompt, x_sample, cache_k, cache_v, w_in, w_out, norm_in, rel_bias, sinks,
           norm_attn, norm_gmlp, ln_v_g, ln_v_b, w_spatial, b_spatial, norm_final):
    bp, seq, _ = x_prompt.shape
    bs, t_len, _ = x_sample.shape
    n_past = cache_k.shape[2]
    assert seq % (TILES_PER_STEP * TM) == 0 and TM % PAIR_ROWS == 0 and TM % GMLP_CHUNK == 0 and TM >= WINDOW
    assert bs % NB == 0 and t_len <= GMLP_CHUNK and n_past >= t_len
    assert TM % ROW_BLOCK == 0 and (NB * t_len) % ROW_BLOCK == 0
    assert (TM // PAIR_ROWS) * N_KV_HEADS == 4 * (D_GMLP // PROJ_COLS)

    bias_p = _prompt_bias_table(rel_bias)
    bias_s = _sample_bias_table(rel_bias, t_len, n_past)
    w_in_b = w_in.astype(BF16)
    w_out_b = w_out.astype(BF16)
    vecs = (norm_in[:, None], norm_attn[:, None], norm_gmlp[:, None],
            ln_v_g[:, None], ln_v_b[:, None], norm_final[None])
    wsp_p, bsp_p = _pair_spatial(w_spatial, b_spatial, GMLP_CHUNK)
    wsp_s, bsp_s = _pair_spatial(w_spatial, b_spatial, t_len)
    ck = cache_k.reshape(DEPTH, bs, n_past, D_KV)
    cv = cache_v.reshape(DEPTH, bs, n_past, D_KV)

    xp = x_prompt
    xs = x_sample.reshape(bs * t_len, D_MODEL)
    kp, vp, ks, vs, vgs = [], [], [], [], []
    for l in range(DEPTH):
        xp, k_l, v_l = _prompt_layer(l, xp, sinks, w_in_b, w_out_b, wsp_p, bsp_p, bias_p, vecs)
        kp.append(k_l)
        vp.append(v_l)
        xs, nk, nv, vg = _sample_layer(l, xs, ck, cv, sinks, w_in_b, w_out_b, wsp_s, bsp_s,
                                       bias_s, vecs, t_len)
        ks.append(nk)
        vs.append(nv)
        vgs.append(vg)

    def heads(rows, n):
        return jnp.stack(rows).reshape(DEPTH, n, -1, N_KV_HEADS, HEAD_DIM)

    return (xp, xs.reshape(bs, t_len, D_MODEL),
            heads(kp, bp), heads(vp, bp), heads(ks, bs), heads(vs, bs),
            jnp.stack(vgs).reshape(DEPTH, bs, t_len, N_GROUPS_B, HEAD_DIM))
```

```python
import functools
import math

import jax
import jax.numpy as jnp
from jax import lax
from jax.experimental import pallas as pl
from jax.experimental.pallas import tpu as pltpu

D_MODEL = 2048
DEPTH = 4
CHUNK = 64
HEAD_DIM = 64
D_ATTN = 1024
D_GMLP = 1024
N_HEADS = 16
N_KV_HEADS = 4
GQA_GROUP = 4
D_KV = 256
WINDOW = 128
N_BUCKETS = 32
MAX_DISTANCE = 128
GMLP_CHUNK = 128
N_GROUPS_B = 16
D_PROJ = 5632
EPS = 1e-6
NEG_INF = -1e30

O_Q, O_K, O_V, O_GA, O_U, O_VG, O_GB = 0, 1024, 1280, 1536, 2560, 3584, 4608

LANES = 128
BF16_ROWS = 16
N_PAIRS = N_GROUPS_B // 2
PAIR_ROWS = 2 * CHUNK
PAIR_KEYS = 4 * CHUNK
TM = 256
TILES_PER_STEP = 2
NB = 4
PROJ_COLS = 512
ROW_BLOCK = 32
VMEM_LIMIT = 58 * 1024 * 1024
VMEM_LIMIT_PROMPT = 61 * 1024 * 1024

F32 = jnp.float32
BF16 = jnp.bfloat16


def _rel_bucket(rel):
    nb = N_BUCKETS // 2
    max_exact = nb // 2
    base = jnp.where(rel > 0, nb, 0)
    n = jnp.abs(rel)
    nf = jnp.maximum(n, 1).astype(F32)
    large = max_exact + (jnp.log(nf / max_exact) / math.log(MAX_DISTANCE / max_exact)
                         * (nb - max_exact)).astype(jnp.int32)
    large = jnp.minimum(large, nb - 1)
    return base + jnp.where(n < max_exact, n, large)


def _band_bias(rel_bias, n_q, n_k, n_past):
    n = n_q + n_k - 1
    rel = jnp.arange(n) - (n_q - 1) - n_past
    v = rel_bias[_rel_bucket(rel)].astype(F32).T
    w = jnp.concatenate([v, jnp.zeros((N_HEADS, 1), F32)], axis=1)
    band = jnp.tile(w, (1, n_q))[:, :n_q * n].reshape(N_HEADS, n_q, n)
    return band[:, :, n_q - 1:]


def _prompt_bias_table(rel_bias):
    b = _band_bias(rel_bias, PAIR_ROWS, PAIR_KEYS, WINDOW)
    b = b.reshape(N_KV_HEADS, GQA_GROUP, PAIR_ROWS, PAIR_KEYS).transpose(0, 3, 1, 2)
    r = jnp.arange(PAIR_ROWS)[None, :]
    j = jnp.arange(PAIR_KEYS)[:, None]
    qc = r // CHUNK
    kc = j // CHUNK - 2
    in_window = (kc <= qc) & (kc >= qc - 2)
    t = jnp.where(in_window[None, :, None, :], b, NEG_INF)
    return t.reshape(N_KV_HEADS, PAIR_KEYS, GQA_GROUP * PAIR_ROWS)


def _sample_bias_table(rel_bias, n_q, n_past):
    b = _band_bias(rel_bias, n_q, n_past + n_q, n_past)
    return jnp.swapaxes(b.reshape(N_KV_HEADS, GQA_GROUP * n_q, n_past + n_q), 1, 2)


def _pair_spatial(w_s, b_s, rows):
    depth = w_s.shape[0]
    w = w_s[:, :, :rows, :rows].reshape(depth, N_PAIRS, 2, rows, rows)
    w = jnp.transpose(w, (0, 1, 3, 2, 4)).reshape(depth, N_PAIRS, rows, 2 * rows)
    b = b_s[:, :, :rows].reshape(depth, N_PAIRS, 2, rows)
    b = jnp.repeat(jnp.transpose(b, (0, 1, 3, 2)), LANES // 2, axis=3)
    return w, b


def _silu(x):
    return x / (1.0 + jnp.exp(-x))


def _gelu(x):
    return 0.5 * x * (1.0 + lax.erf(x * math.sqrt(0.5)))


def _rms_to_bf16(x_ref, g_ref, h_s):
    for r in range(0, x_ref.shape[0], ROW_BLOCK):
        x = x_ref[r:r + ROW_BLOCK, :]
        inv = lax.rsqrt(jnp.mean(x * x, axis=-1, keepdims=True) + EPS)
        h_s[r:r + ROW_BLOCK, :] = (x * inv * g_ref[...]).astype(BF16)


def _proj(h_s, w_in_ref, off, width):
    return jnp.dot(h_s[...], w_in_ref[:, off:off + width], preferred_element_type=F32)


def _proj_kvq(h_s, w_in_ref, q_s, q_transposed=False):
    kv = _proj(h_s, w_in_ref, O_K, 2 * D_KV)
    for c in range(0, D_ATTN, PROJ_COLS):
        q = _proj(h_s, w_in_ref, O_Q + c, PROJ_COLS) * (HEAD_DIM ** -0.5)
        if q_transposed:
            q_s[c:c + PROJ_COLS, :] = q.T.astype(BF16)
        else:
            q_s[:, c:c + PROJ_COLS] = q.astype(BF16)
    return kv[:, :D_KV], kv[:, D_KV:]


def _act_chunks(h_s, w_in_ref, vgf_s, u_s, gb_s, ga_s):
    def make(dst, off, act, c):
        def run():
            dst[:, c:c + PROJ_COLS] = act(_proj(h_s, w_in_ref, off + c, PROJ_COLS))
        return run
    runs = [make(dst, off, act, c)
            for dst, off, act in ((ga_s, O_GA, _silu), (vgf_s, O_VG, _gelu), (u_s, O_U, _gelu),
                                  (gb_s, O_GB, _silu))
            for c in range(0, D_GMLP, PROJ_COLS)]
    n = D_GMLP // PROJ_COLS
    return runs[:1] + runs[n:] + runs[1:n]


def _layernorm_split(vgf_s, ln_g_ref, ln_b_ref, vlo_s, vhi_s):
    low = (lax.broadcasted_iota(jnp.int32, (ROW_BLOCK, D_GMLP), 1) % LANES) < (LANES // 2)
    for r in range(0, vgf_s.shape[0], ROW_BLOCK):
        g = vgf_s[r:r + ROW_BLOCK, :]
        gc = g - jnp.mean(g, axis=-1, keepdims=True)
        y = gc * lax.rsqrt(jnp.mean(gc * gc, axis=-1, keepdims=True) + EPS)
        y = y * ln_g_ref[...] + ln_b_ref[...]
        vgf_s[r:r + ROW_BLOCK, :] = y
        vlo_s[r:r + ROW_BLOCK, :] = jnp.where(low, y, 0.0).astype(BF16)
        vhi_s[r:r + ROW_BLOCK, :] = jnp.where(low, 0.0, y).astype(BF16)


def _scores_t(keys, q_rows):
    return lax.dot_general(keys, q_rows, (((1,), (1,)), ((), ())), preferred_element_type=F32)


def _zero_after(row):
    u = lax.bitcast_convert_type(row, jnp.uint32)
    u = lax.shift_right_logical(lax.shift_right_logical(u, jnp.uint32(16)), jnp.uint32(16))
    return lax.bitcast_convert_type(u, F32)


def _softmax_pv_t(s, vals_t, bias_t, sink_row, after_row, mask_past=None):
    s = s + bias_t
    if mask_past is not None:
        s = jnp.concatenate([jnp.where(mask_past, NEG_INF, s[:WINDOW]), s[WINDOW:]], axis=0)
    m = jnp.maximum(jnp.max(s, axis=0, keepdims=True), sink_row)
    e = jnp.exp(s - m)
    l = jnp.sum(e, axis=0, keepdims=True) + jnp.exp(sink_row - m)
    tail = e.shape[0] - BF16_ROWS
    p = jnp.concatenate([e[:tail].astype(BF16),
                         (e[tail:] + _zero_after(after_row)).astype(BF16)], axis=0)
    o = jnp.dot(vals_t, p, preferred_element_type=F32)
    return o * (1.0 / l)


def _sink_row(sinks_ref, layer, kvh, n):
    shape = (1, GQA_GROUP * n)
    g = lax.broadcasted_iota(jnp.int32, shape, 1) // n
    out = jnp.full(shape, sinks_ref[layer, kvh * GQA_GROUP], F32)
    for i in range(1, GQA_GROUP):
        out = jnp.where(g == i, sinks_ref[layer, kvh * GQA_GROUP + i], out)
    return out


def _masked_spatial(wsp_ref, j, rows):
    w = wsp_ref[j]
    i = lax.broadcasted_iota(jnp.int32, w.shape, 0)
    c = lax.broadcasted_iota(jnp.int32, w.shape, 1) % rows
    return jnp.where(c <= i, w, 0.0).astype(BF16)


def _gated_norm(src_s, gate_s, g_ref, y_s, col0):
    rows, width = src_s.shape
    for r in range(0, rows, ROW_BLOCK):
        a = src_s[r:r + ROW_BLOCK, :]
        inv = lax.rsqrt(jnp.mean(a * a, axis=-1, keepdims=True) + EPS)
        y_s[r:r + ROW_BLOCK, col0:col0 + width] = (
            a * inv * g_ref[...] * gate_s[r:r + ROW_BLOCK, :]).astype(BF16)


def _out_gmlp_half(x_ref, w_out_ref, o_ref, y_s, col_lo=0, col_hi=D_MODEL):
    for c in range(col_lo, col_hi, PROJ_COLS):
        o_ref[:, c:c + PROJ_COLS] = x_ref[:, c:c + PROJ_COLS] + jnp.dot(
            y_s[:, D_ATTN:], w_out_ref[D_ATTN:, c:c + PROJ_COLS], preferred_element_type=F32)


def _out_attn_half(w_out_ref, g_fin_ref, o_ref, y_s, final):
    rows = o_ref.shape[0]
    for c in range(0, D_MODEL, PROJ_COLS):
        o_ref[:, c:c + PROJ_COLS] = o_ref[:, c:c + PROJ_COLS] + jnp.dot(
            y_s[:, :D_ATTN], w_out_ref[:D_ATTN, c:c + PROJ_COLS], preferred_element_type=F32)
    if final:
        for r in range(0, rows, ROW_BLOCK):
            xn = o_ref[r:r + ROW_BLOCK, :]
            inv = lax.rsqrt(jnp.mean(xn * xn, axis=-1, keepdims=True) + EPS)
            o_ref[r:r + ROW_BLOCK, :] = xn * inv * g_fin_ref[...]


def _prompt_tile(sinks_ref, x_ref, w_in_ref, w_out_ref, wsp_ref, bsp_ref, bias_ref,
                 g_in_ref, g_attn_ref, g_gmlp_ref, ln_g_ref, ln_b_ref, g_fin_ref,
                 o_ref, kv_out_refs,
                 h_s, q_s, k_ext, vt_ext, ga_s, gb_s, u_s, vgf_s, vlo_s, vhi_s, ao_s, y_s,
                 *, layer, first):
    final = layer == DEPTH - 1
    _rms_to_bf16(x_ref, g_in_ref, h_s)
    k, v = _proj_kvq(h_s, w_in_ref, q_s, q_transposed=True)
    k_ext[WINDOW:, :] = k.astype(BF16)
    vt_ext[:, WINDOW:] = v.T.astype(BF16)
    if kv_out_refs is not None:
        kv_out_refs[0][...] = k[TM - WINDOW:, :]
        kv_out_refs[1][...] = v[TM - WINDOW:, :]

    def gmlp_mix():
        for j in range(N_PAIRS):
            w = _masked_spatial(wsp_ref, j, GMLP_CHUNK)
            cols = slice(j * LANES, (j + 1) * LANES)
            for c in range(TM // GMLP_CHUNK):
                rws = slice(c * GMLP_CHUNK, (c + 1) * GMLP_CHUNK)
                rhs = jnp.concatenate([vlo_s[rws, cols], vhi_s[rws, cols]], axis=0)
                mix = jnp.dot(w, rhs, preferred_element_type=F32) + bsp_ref[j]
                u_s[rws, cols] = u_s[rws, cols] * mix

    def store_block(p, kvh, o_t):
        o_gd = jnp.concatenate(
            [o_t[:, g * PAIR_ROWS:(g + 1) * PAIR_ROWS] for g in range(GQA_GROUP)], axis=0)
        ao_s[p * PAIR_ROWS:(p + 1) * PAIR_ROWS, kvh * D_KV:(kvh + 1) * D_KV] = o_gd.T

    def scores(i):
        p, kvh = divmod(i, N_KV_HEADS)
        r0 = p * PAIR_ROWS
        c0 = kvh * GQA_GROUP * HEAD_DIM
        q_t = jnp.concatenate(
            [q_s[c0 + g * HEAD_DIM:c0 + (g + 1) * HEAD_DIM, r0:r0 + PAIR_ROWS]
             for g in range(GQA_GROUP)], axis=1)
        zeros = jnp.zeros_like(q_t)
        q_t = jnp.concatenate([q_t, zeros] if kvh % 2 == 0 else [zeros, q_t], axis=0)
        lanes = slice((kvh // 2) * LANES, (kvh // 2 + 1) * LANES)
        return jnp.dot(k_ext[r0:r0 + PAIR_KEYS, lanes], q_t, preferred_element_type=F32)

    chunks = _act_chunks(h_s, w_in_ref, vgf_s, u_s, gb_s, ga_s)
    n_blocks = (TM // PAIR_ROWS) * N_KV_HEADS
    chunks[0]()
    s_next = scores(0)
    for i in range(n_blocks):
        p, kvh = divmod(i, N_KV_HEADS)
        r0 = p * PAIR_ROWS
        hc = slice(kvh * HEAD_DIM, (kvh + 1) * HEAD_DIM)
        s = s_next
        if i + 1 < n_blocks:
            s_next = scores(i + 1)
            after_row = s_next[0:1, :]
            chunks[i + 1]()
        else:
            _out_gmlp_half(x_ref, w_out_ref, o_ref, y_s, 0, 2 * PROJ_COLS)
            after_row = o_ref[0:1, 0:PROJ_COLS]
        store_block(p, kvh, _softmax_pv_t(
            s, vt_ext[hc, r0:r0 + PAIR_KEYS], bias_ref[kvh],
            _sink_row(sinks_ref, layer, kvh, PAIR_ROWS), after_row,
            mask_past=first if p == 0 else None))
        if i == 1:
            _layernorm_split(vgf_s, ln_g_ref, ln_b_ref, vlo_s, vhi_s)
        if i == 3:
            gmlp_mix()
        if i == 5:
            _gated_norm(u_s, gb_s, g_gmlp_ref, y_s, D_ATTN)
    _out_gmlp_half(x_ref, w_out_ref, o_ref, y_s, 2 * PROJ_COLS, D_MODEL)

    k_ext[0:WINDOW, :] = k_ext[TM:TM + WINDOW, :]
    vt_ext[:, 0:WINDOW] = vt_ext[:, TM:TM + WINDOW]

    _gated_norm(ao_s, ga_s, g_attn_ref, y_s, 0)
    _out_attn_half(w_out_ref, g_fin_ref, o_ref, y_s, final)


def _prompt_kernel(sinks_ref, x_ref, *refs, layer, cast_next):
    weights, refs = refs[:11], refs[11:]
    if cast_next:
        (w_in_f32, w_out_f32), refs = refs[:2], refs[2:]
    (o_ref, k_out_ref, v_out_ref), refs = refs[:3], refs[3:]
    if cast_next:
        (w_in_bf16, w_out_bf16), refs = refs[:2], refs[2:]
        w_in_bf16[...] = w_in_f32[...].astype(BF16)
        w_out_bf16[...] = w_out_f32[...].astype(BF16)
    scratch = refs
    k_ext, vt_ext = scratch[2], scratch[3]
    t = pl.program_id(1)

    @pl.when(t == 0)
    def _():
        k_ext[0:WINDOW, :] = jnp.zeros((WINDOW, D_KV), BF16)
        vt_ext[:, 0:WINDOW] = jnp.zeros((D_KV, WINDOW), BF16)

    for sub in range(TILES_PER_STEP):
        rows = slice(sub * TM, (sub + 1) * TM)
        last = sub == TILES_PER_STEP - 1
        _prompt_tile(sinks_ref, x_ref.at[rows, :], *weights, o_ref.at[rows, :],
                     (k_out_ref, v_out_ref) if last else None, *scratch,
                     layer=layer, first=(t == 0) if sub == 0 else None)


def _sample_kernel(sinks_ref, x_ref, ck_ref, cv_ref, w_in_ref, w_out_ref, wsp_ref, bsp_ref, bias_ref,
                   g_in_ref, g_attn_ref, g_gmlp_ref, ln_g_ref, ln_b_ref, g_fin_ref,
                   o_ref, nk_ref, nv_ref, vg_ref,
                   h_s, q_s, ga_s, gb_s, u_s, vlo_s, vhi_s, ao_s, y_s,
                   *, layer, t_len):
    final = layer == DEPTH - 1
    n_past = ck_ref.shape[1]
    q_cols = GQA_GROUP * t_len
    _rms_to_bf16(x_ref, g_in_ref, h_s)
    k, v = _proj_kvq(h_s, w_in_ref, q_s)

    nk_ref[:, 0:n_past - t_len, :] = ck_ref[:, t_len:, :]
    nv_ref[:, 0:n_past - t_len, :] = cv_ref[:, t_len:, :]
    nk_ref[:, n_past - t_len:, :] = k.reshape(NB, t_len, D_KV)
    nv_ref[:, n_past - t_len:, :] = v.reshape(NB, t_len, D_KV)

    kb = k.astype(BF16)
    vt_new = v.T.astype(BF16)
    vt_past = [cv_ref[b].T.astype(BF16) for b in range(NB)]

    def gmlp_mix():
        for j in range(N_PAIRS):
            w = _masked_spatial(wsp_ref, j, t_len)
            cols = slice(j * LANES, (j + 1) * LANES)
            rhs = jnp.concatenate(
                [jnp.concatenate([vlo_s[b * t_len:(b + 1) * t_len, cols],
                                  vhi_s[b * t_len:(b + 1) * t_len, cols]], axis=0)
                 for b in range(NB)], axis=1)
            mix = jnp.dot(w, rhs, preferred_element_type=F32)
            for b in range(NB):
                rws = slice(b * t_len, (b + 1) * t_len)
                u_s[rws, cols] = u_s[rws, cols] * (mix[:, b * LANES:(b + 1) * LANES] + bsp_ref[j])

    def scores(i):
        b, kvh = divmod(i, N_KV_HEADS)
        r0 = b * t_len
        c0 = kvh * GQA_GROUP * HEAD_DIM
        hc = slice(kvh * HEAD_DIM, (kvh + 1) * HEAD_DIM)
        q_rows = jnp.concatenate(
            [q_s[r0:r0 + t_len, c0 + g * HEAD_DIM:c0 + (g + 1) * HEAD_DIM]
             for g in range(GQA_GROUP)], axis=0)
        keys = jnp.concatenate([ck_ref[b, :, hc].astype(BF16), kb[r0:r0 + t_len, hc]], axis=0)
        return _scores_t(keys, q_rows)

    def finish(i, s, after_row):
        b, kvh = divmod(i, N_KV_HEADS)
        r0 = b * t_len
        c0 = kvh * GQA_GROUP * HEAD_DIM
        hc = slice(kvh * HEAD_DIM, (kvh + 1) * HEAD_DIM)
        vals_t = jnp.concatenate([vt_past[b][hc, :], vt_new[hc, r0:r0 + t_len]], axis=1)
        o = _softmax_pv_t(s, vals_t, bias_ref[kvh],
                          _sink_row(sinks_ref, layer, kvh, t_len), after_row).T
        for g in range(GQA_GROUP):
            cg = c0 + g * HEAD_DIM
            ao_s[r0:r0 + t_len, cg:cg + HEAD_DIM] = o[g * t_len:(g + 1) * t_len, :]

    chunks = _act_chunks(h_s, w_in_ref, vg_ref, u_s, gb_s, ga_s)
    per = (NB * N_KV_HEADS) // len(chunks)
    chunks[0]()
    s_next = [scores(i) for i in range(per)]
    for j in range(len(chunks)):
        s_cur = s_next
        if j + 1 < len(chunks):
            s_next = [scores((j + 1) * per + i) for i in range(per)]
            after = [s[0:1, :] for s in s_next]
            chunks[j + 1]()
        else:
            _out_gmlp_half(x_ref, w_out_ref, o_ref, y_s, 0, PROJ_COLS)
            after = [o_ref[0:1, 0:q_cols]] * per
        for i in range(per):
            finish(j * per + i, s_cur[i], after[i])
        if j == 1:
            _layernorm_split(vg_ref, ln_g_ref, ln_b_ref, vlo_s, vhi_s)
        if j == 3:
            gmlp_mix()
        if j == 5:
            _gated_norm(u_s, gb_s, g_gmlp_ref, y_s, D_ATTN)
    _out_gmlp_half(x_ref, w_out_ref, o_ref, y_s, PROJ_COLS, D_MODEL)

    _gated_norm(ao_s, ga_s, g_attn_ref, y_s, 0)
    _out_attn_half(w_out_ref, g_fin_ref, o_ref, y_s, final)


def _resident(shape):
    nd = len(shape)
    return pl.BlockSpec(shape, lambda *_: (0,) * nd, pipeline_mode=pl.Buffered(1))


def _layer_slice(arr, layer):
    tail = arr.shape[1:]
    return pl.BlockSpec((None,) + tail, lambda *_: (layer,) + (0,) * len(tail),
                        pipeline_mode=pl.Buffered(1))


def _weight_specs(layer, w_in, w_out, wsp, bsp, bias, vecs):
    return ([_layer_slice(w_in, 0), _layer_slice(w_out, 0), _layer_slice(wsp, layer),
             _layer_slice(bsp, layer), _resident(bias.shape)]
            + [_layer_slice(vec, layer) for vec in vecs[:-1]] + [_resident(vecs[-1].shape)])


def _prompt_layer(layer, x, sinks, w_in, w_out, wsp, bsp, bias, vecs, next_f32=None):
    batch, seq, _ = x.shape
    n_t = seq // (TILES_PER_STEP * TM)
    kv_shape = jax.ShapeDtypeStruct((batch, WINDOW, D_KV), F32)
    kv_spec = pl.BlockSpec((None, WINDOW, D_KV), lambda b, t: (b, 0, 0))
    x_spec = pl.BlockSpec((None, TILES_PER_STEP * TM, D_MODEL), lambda b, t: (b, t, 0))
    final = layer == DEPTH - 1
    in_specs = ([pl.BlockSpec(memory_space=pltpu.SMEM), x_spec]
                + _weight_specs(layer, w_in, w_out, wsp, bsp, bias, vecs))
    out_specs = [x_spec, kv_spec, kv_spec]
    out_shape = [jax.ShapeDtypeStruct(x.shape, F32), kv_shape, kv_shape]
    args = [sinks, x, w_in, w_out, wsp, bsp, bias, *vecs]
    if next_f32 is not None:
        rows = D_MODEL // (batch * n_t)
        assert rows * batch * n_t == D_MODEL and rows % BF16_ROWS == 0
        for w in next_f32:
            cols = w.shape[-1]
            in_specs.append(pl.BlockSpec((None, rows, cols),
                                         lambda b, t: (layer + 1, b * n_t + t, 0)))
            out_specs.append(pl.BlockSpec((None, rows, cols), lambda b, t: (0, b * n_t + t, 0)))
            out_shape.append(jax.ShapeDtypeStruct((1, D_MODEL, cols), BF16))
            args.append(w)
    return pl.pallas_call(
        functools.partial(_prompt_kernel, layer=layer, cast_next=next_f32 is not None),
        grid=(batch, n_t),
        in_specs=in_specs,
        out_specs=out_specs,
        out_shape=out_shape,
        scratch_shapes=[
            pltpu.VMEM((TM, D_MODEL), BF16),
            pltpu.VMEM((D_ATTN, TM), BF16),
            pltpu.VMEM((WINDOW + TM, D_KV), BF16),
            pltpu.VMEM((D_KV, WINDOW + TM), BF16),
            pltpu.VMEM((TM, D_ATTN), F32),
            pltpu.VMEM((TM, D_GMLP), F32),
            pltpu.VMEM((TM, D_GMLP), F32),
            pltpu.VMEM((TM, D_GMLP), F32),
            pltpu.VMEM((TM, D_GMLP), BF16),
            pltpu.VMEM((TM, D_GMLP), BF16),
            pltpu.VMEM((TM, D_ATTN), F32),
            pltpu.VMEM((TM, D_MODEL), BF16),
        ],
        compiler_params=pltpu.CompilerParams(
            dimension_semantics=("arbitrary", "arbitrary"), vmem_limit_bytes=VMEM_LIMIT_PROMPT),
        name="prompt_layer_final" if final else "prompt_layer",
    )(*args)


def _sample_layer(layer, x, ck, cv, sinks, w_in, w_out, wsp, bsp, bias, vecs, t_len):
    rows = x.shape[0]
    _, n_streams, n_past, _ = ck.shape
    step_rows = NB * t_len
    x_spec = pl.BlockSpec((step_rows, D_MODEL), lambda i: (i, 0))
    cin_spec = pl.BlockSpec((None, NB, n_past, D_KV), lambda i: (layer, i, 0, 0))
    cout_spec = pl.BlockSpec((NB, n_past, D_KV), lambda i: (i, 0, 0))
    vg_spec = pl.BlockSpec((step_rows, D_GMLP), lambda i: (i, 0))
    cache_shape = jax.ShapeDtypeStruct((n_streams, n_past, D_KV), F32)
    final = layer == DEPTH - 1
    return pl.pallas_call(
        functools.partial(_sample_kernel, layer=layer, t_len=t_len),
        grid=(n_streams // NB,),
        in_specs=[pl.BlockSpec(memory_space=pltpu.SMEM), x_spec, cin_spec, cin_spec]
        + _weight_specs(layer, w_in, w_out, wsp, bsp, bias, vecs),
        out_specs=[x_spec, cout_spec, cout_spec, vg_spec],
        out_shape=[jax.ShapeDtypeStruct(x.shape, F32), cache_shape, cache_shape,
                   jax.ShapeDtypeStruct((rows, D_GMLP), F32)],
        scratch_shapes=[
            pltpu.VMEM((step_rows, D_MODEL), BF16),
            pltpu.VMEM((step_rows, D_ATTN), BF16),
            pltpu.VMEM((step_rows, D_ATTN), F32),
            pltpu.VMEM((step_rows, D_GMLP), F32),
            pltpu.VMEM((step_rows, D_GMLP), F32),
            pltpu.VMEM((step_rows, D_GMLP), BF16),
            pltpu.VMEM((step_rows, D_GMLP), BF16),
            pltpu.VMEM((step_rows, D_ATTN), F32),
            pltpu.VMEM((step_rows, D_MODEL), BF16),
        ],
        compiler_params=pltpu.CompilerParams(
            dimension_semantics=("arbitrary",), vmem_limit_bytes=VMEM_LIMIT),
        name="sample_layer_final" if final else "sample_layer",
    )(sinks, x, ck, cv, w_in, w_out, wsp, bsp, bias, *vecs)


def kernel(x_prompt, x_sample, cache_k, cache_v, w_in, w_out, norm_in, rel_bias, sinks,
           norm_attn, norm_gmlp, ln_v_g, ln_v_b, w_spatial, b_spatial, norm_final):
    bp, seq, _ = x_prompt.shape
    bs, t_len, _ = x_sample.shape
    n_past = cache_k.shape[2]
    assert seq % (TILES_PER_STEP * TM) == 0 and TM % PAIR_ROWS == 0 and TM % GMLP_CHUNK == 0 and TM >= WINDOW
    assert bs % NB == 0 and t_len <= GMLP_CHUNK and n_past >= t_len
    assert TM % ROW_BLOCK == 0 and (NB * t_len) % ROW_BLOCK == 0
    assert (TM // PAIR_ROWS) * N_KV_HEADS == 4 * (D_GMLP // PROJ_COLS)

    bias_p = _prompt_bias_table(rel_bias)
    bias_s = _sample_bias_table(rel_bias, t_len, n_past)
    w_in_b = w_in[:1].astype(BF16)
    w_out_b = w_out[:1].astype(BF16)
    vecs = (norm_in[:, None], norm_attn[:, None], norm_gmlp[:, None],
            ln_v_g[:, None], ln_v_b[:, None], norm_final[None])
    wsp_p, bsp_p = _pair_spatial(w_spatial, b_spatial, GMLP_CHUNK)
    wsp_s, bsp_s = _pair_spatial(w_spatial, b_spatial, t_len)
    ck = cache_k.reshape(DEPTH, bs, n_past, D_KV)
    cv = cache_v.reshape(DEPTH, bs, n_past, D_KV)

    xp = x_prompt
    xs = x_sample.reshape(bs * t_len, D_MODEL)
    kp, vp, ks, vs, vgs = [], [], [], [], []
    for l in range(DEPTH):
        outs = _prompt_layer(l, xp, sinks, w_in_b, w_out_b, wsp_p, bsp_p, bias_p, vecs,
                             next_f32=(w_in, w_out) if l + 1 < DEPTH else None)
        xp, k_l, v_l = outs[:3]
        kp.append(k_l)
        vp.append(v_l)
        xs, nk, nv, vg = _sample_layer(l, xs, ck, cv, sinks, w_in_b, w_out_b, wsp_s, bsp_s,
                                       bias_s, vecs, t_len)
        if l + 1 < DEPTH:
            w_in_b, w_out_b = outs[3:]
        ks.append(nk)
        vs.append(nv)
        vgs.append(vg)

    def heads(rows, n):
        return jnp.stack(rows).reshape(DEPTH, n, -1, N_KV_HEADS, HEAD_DIM)

    return (xp, xs.reshape(bs, t_len, D_MODEL),
            heads(kp, bp), heads(vp, bp), heads(ks, bs), heads(vs, bs),
            jnp.stack(vgs).reshape(DEPTH, bs, t_len, N_GROUPS_B, HEAD_DIM))
```

```python
import functools
import math

import jax
import jax.numpy as jnp
from jax import lax
from jax.experimental import pallas as pl
from jax.experimental.pallas import tpu as pltpu

D_MODEL = 2048
DEPTH = 4
CHUNK = 64
HEAD_DIM = 64
D_ATTN = 1024
D_GMLP = 1024
N_HEADS = 16
N_KV_HEADS = 4
GQA_GROUP = 4
D_KV = 256
WINDOW = 128
N_BUCKETS = 32
MAX_DISTANCE = 128
GMLP_CHUNK = 128
N_GROUPS_B = 16
D_PROJ = 5632
EPS = 1e-6
NEG_INF = -1e30

O_Q, O_K, O_V, O_GA, O_U, O_VG, O_GB = 0, 1024, 1280, 1536, 2560, 3584, 4608

LANES = 128
BF16_ROWS = 16
N_PAIRS = N_GROUPS_B // 2
PAIR_ROWS = 2 * CHUNK
PAIR_KEYS = 4 * CHUNK
TM = 256
TILES_PER_STEP = 2
NB = 4
PROJ_COLS = 512
ROW_BLOCK = 32
VMEM_LIMIT = 58 * 1024 * 1024
VMEM_LIMIT_PROMPT = 61 * 1024 * 1024

F32 = jnp.float32
BF16 = jnp.bfloat16


def _rel_bucket(rel):
    nb = N_BUCKETS // 2
    max_exact = nb // 2
    base = jnp.where(rel > 0, nb, 0)
    n = jnp.abs(rel)
    nf = jnp.maximum(n, 1).astype(F32)
    large = max_exact + (jnp.log(nf / max_exact) / math.log(MAX_DISTANCE / max_exact)
                         * (nb - max_exact)).astype(jnp.int32)
    large = jnp.minimum(large, nb - 1)
    return base + jnp.where(n < max_exact, n, large)


def _band_bias(rel_bias, n_q, n_k, n_past):
    n = n_q + n_k - 1
    rel = jnp.arange(n) - (n_q - 1) - n_past
    v = rel_bias[_rel_bucket(rel)].astype(F32).T
    w = jnp.concatenate([v, jnp.zeros((N_HEADS, 1), F32)], axis=1)
    band = jnp.tile(w, (1, n_q))[:, :n_q * n].reshape(N_HEADS, n_q, n)
    return band[:, :, n_q - 1:]


def _prompt_bias_table(rel_bias):
    b = _band_bias(rel_bias, PAIR_ROWS, PAIR_KEYS, WINDOW)
    b = b.reshape(N_KV_HEADS, GQA_GROUP, PAIR_ROWS, PAIR_KEYS).transpose(0, 3, 1, 2)
    r = jnp.arange(PAIR_ROWS)[None, :]
    j = jnp.arange(PAIR_KEYS)[:, None]
    qc = r // CHUNK
    kc = j // CHUNK - 2
    in_window = (kc <= qc) & (kc >= qc - 2)
    t = jnp.where(in_window[None, :, None, :], b, NEG_INF)
    return t.reshape(N_KV_HEADS, PAIR_KEYS, GQA_GROUP * PAIR_ROWS)


def _sample_bias_table(rel_bias, n_q, n_past):
    b = _band_bias(rel_bias, n_q, n_past + n_q, n_past)
    return jnp.swapaxes(b.reshape(N_KV_HEADS, GQA_GROUP * n_q, n_past + n_q), 1, 2)


def _pair_spatial(w_s, b_s, rows):
    depth = w_s.shape[0]
    w = w_s[:, :, :rows, :rows].reshape(depth, N_PAIRS, 2, rows, rows)
    w = jnp.transpose(w, (0, 1, 3, 2, 4)).reshape(depth, N_PAIRS, rows, 2 * rows)
    b = b_s[:, :, :rows].reshape(depth, N_PAIRS, 2, rows)
    b = jnp.repeat(jnp.transpose(b, (0, 1, 3, 2)), LANES // 2, axis=3)
    return w, b


def _silu(x):
    return x / (1.0 + jnp.exp(-x))


def _gelu(x):
    return 0.5 * x * (1.0 + lax.erf(x * math.sqrt(0.5)))


def _rms_to_bf16(x_ref, g_ref, h_s):
    for r in range(0, x_ref.shape[0], ROW_BLOCK):
        x = x_ref[r:r + ROW_BLOCK, :]
        inv = lax.rsqrt(jnp.mean(x * x, axis=-1, keepdims=True) + EPS)
        h_s[r:r + ROW_BLOCK, :] = (x * inv * g_ref[...]).astype(BF16)


def _proj(h_s, w_in_ref, off, width):
    return jnp.dot(h_s[...], w_in_ref[:, off:off + width], preferred_element_type=F32)


SCORE_SCALE = HEAD_DIM ** -0.5


def _proj_kvq(h_s, w_in_ref, q_s, q_transposed=False):
    kv = _proj(h_s, w_in_ref, O_K, 2 * D_KV)
    for c in range(0, D_ATTN, PROJ_COLS):
        q = _proj(h_s, w_in_ref, O_Q + c, PROJ_COLS)
        if q_transposed:
            q_s[c:c + PROJ_COLS, :] = q.T.astype(BF16)
        else:
            q_s[:, c:c + PROJ_COLS] = (q * SCORE_SCALE).astype(BF16)
    return kv[:, :D_KV], kv[:, D_KV:]


def _act_chunks(h_s, w_in_ref, vgf_s, u_s, gb_s, ga_s):
    def make(dst, off, act, c):
        def run():
            dst[:, c:c + PROJ_COLS] = act(_proj(h_s, w_in_ref, off + c, PROJ_COLS))
        return run
    runs = [make(dst, off, act, c)
            for dst, off, act in ((ga_s, O_GA, _silu), (vgf_s, O_VG, _gelu), (u_s, O_U, _gelu),
                                  (gb_s, O_GB, _silu))
            for c in range(0, D_GMLP, PROJ_COLS)]
    n = D_GMLP // PROJ_COLS
    return runs[:1] + runs[n:] + runs[1:n]


def _layernorm_split(vgf_s, ln_g_ref, ln_b_ref, vlo_s, vhi_s):
    low = (lax.broadcasted_iota(jnp.int32, (ROW_BLOCK, D_GMLP), 1) % LANES) < (LANES // 2)
    for r in range(0, vgf_s.shape[0], ROW_BLOCK):
        g = vgf_s[r:r + ROW_BLOCK, :]
        gc = g - jnp.mean(g, axis=-1, keepdims=True)
        y = gc * lax.rsqrt(jnp.mean(gc * gc, axis=-1, keepdims=True) + EPS)
        y = y * ln_g_ref[...] + ln_b_ref[...]
        vgf_s[r:r + ROW_BLOCK, :] = y
        vlo_s[r:r + ROW_BLOCK, :] = jnp.where(low, y, 0.0).astype(BF16)
        vhi_s[r:r + ROW_BLOCK, :] = jnp.where(low, 0.0, y).astype(BF16)


def _scores_t(keys, q_rows):
    return lax.dot_general(keys, q_rows, (((1,), (1,)), ((), ())), preferred_element_type=F32)


def _zero_after(row):
    u = lax.bitcast_convert_type(row, jnp.uint32)
    u = lax.shift_right_logical(lax.shift_right_logical(u, jnp.uint32(16)), jnp.uint32(16))
    return lax.bitcast_convert_type(u, F32)


def _softmax_pv_t(s, vals_t, bias_t, sink_row, after_row, mask_past=None):
    s = s + bias_t
    if mask_past is not None:
        s = jnp.concatenate([jnp.where(mask_past, NEG_INF, s[:WINDOW]), s[WINDOW:]], axis=0)
    m = jnp.maximum(jnp.max(s, axis=0, keepdims=True), sink_row)
    e = jnp.exp(s - m)
    l = jnp.sum(e, axis=0, keepdims=True) + jnp.exp(sink_row - m)
    if after_row is None:
        p = e.astype(BF16)
    else:
        tail = e.shape[0] - BF16_ROWS
        p = jnp.concatenate([e[:tail].astype(BF16),
                             (e[tail:] + _zero_after(after_row)).astype(BF16)], axis=0)
    o = jnp.dot(vals_t, p, preferred_element_type=F32)
    return o * (1.0 / l)


def _sink_row(sinks_ref, layer, kvh, n):
    shape = (1, GQA_GROUP * n)
    g = lax.broadcasted_iota(jnp.int32, shape, 1) // n
    out = jnp.full(shape, sinks_ref[layer, kvh * GQA_GROUP], F32)
    for i in range(1, GQA_GROUP):
        out = jnp.where(g == i, sinks_ref[layer, kvh * GQA_GROUP + i], out)
    return out


def _masked_spatial(wsp_ref, j, rows):
    w = wsp_ref[j]
    i = lax.broadcasted_iota(jnp.int32, w.shape, 0)
    c = lax.broadcasted_iota(jnp.int32, w.shape, 1) % rows
    return jnp.where(c <= i, w, 0.0).astype(BF16)


def _gated_norm(src_s, gate_s, g_ref, y_s, col0):
    rows, width = src_s.shape
    for r in range(0, rows, ROW_BLOCK):
        a = src_s[r:r + ROW_BLOCK, :]
        inv = lax.rsqrt(jnp.mean(a * a, axis=-1, keepdims=True) + EPS)
        y_s[r:r + ROW_BLOCK, col0:col0 + width] = (
            a * inv * g_ref[...] * gate_s[r:r + ROW_BLOCK, :]).astype(BF16)


def _out_gmlp_half(x_ref, w_out_ref, o_ref, y_s, col_lo=0, col_hi=D_MODEL):
    for c in range(col_lo, col_hi, PROJ_COLS):
        o_ref[:, c:c + PROJ_COLS] = x_ref[:, c:c + PROJ_COLS] + jnp.dot(
            y_s[:, D_ATTN:], w_out_ref[D_ATTN:, c:c + PROJ_COLS], preferred_element_type=F32)


def _out_attn_half(w_out_ref, g_fin_ref, o_ref, y_s, final):
    rows = o_ref.shape[0]
    for c in range(0, D_MODEL, PROJ_COLS):
        o_ref[:, c:c + PROJ_COLS] = o_ref[:, c:c + PROJ_COLS] + jnp.dot(
            y_s[:, :D_ATTN], w_out_ref[:D_ATTN, c:c + PROJ_COLS], preferred_element_type=F32)
    if final:
        for r in range(0, rows, ROW_BLOCK):
            xn = o_ref[r:r + ROW_BLOCK, :]
            inv = lax.rsqrt(jnp.mean(xn * xn, axis=-1, keepdims=True) + EPS)
            o_ref[r:r + ROW_BLOCK, :] = xn * inv * g_fin_ref[...]


def _prompt_tile(sinks_ref, x_ref, w_in_ref, w_out_ref, wsp_ref, bsp_ref, bias_ref,
                 g_in_ref, g_attn_ref, g_gmlp_ref, ln_g_ref, ln_b_ref, g_fin_ref,
                 o_ref, kv_out_refs,
                 h_s, q_s, k_ext, vt_ext, ga_s, gb_s, u_s, vgf_s, vlo_s, vhi_s, ao_s, y_s,
                 *, layer, first):
    final = layer == DEPTH - 1
    _rms_to_bf16(x_ref, g_in_ref, h_s)
    k, v = _proj_kvq(h_s, w_in_ref, q_s, q_transposed=True)
    k_ext[WINDOW:, :] = (k * SCORE_SCALE).astype(BF16)
    vt_ext[:, WINDOW:] = v.T.astype(BF16)
    if kv_out_refs is not None:
        kv_out_refs[0][...] = k[TM - WINDOW:, :]
        kv_out_refs[1][...] = v[TM - WINDOW:, :]

    def gmlp_mix():
        for j in range(N_PAIRS):
            w = _masked_spatial(wsp_ref, j, GMLP_CHUNK)
            cols = slice(j * LANES, (j + 1) * LANES)
            for c in range(TM // GMLP_CHUNK):
                rws = slice(c * GMLP_CHUNK, (c + 1) * GMLP_CHUNK)
                rhs = jnp.concatenate([vlo_s[rws, cols], vhi_s[rws, cols]], axis=0)
                mix = jnp.dot(w, rhs, preferred_element_type=F32) + bsp_ref[j]
                u_s[rws, cols] = u_s[rws, cols] * mix

    def store_block(p, kvh, o_t):
        o_gd = jnp.concatenate(
            [o_t[:, g * PAIR_ROWS:(g + 1) * PAIR_ROWS] for g in range(GQA_GROUP)], axis=0)
        ao_s[p * PAIR_ROWS:(p + 1) * PAIR_ROWS, kvh * D_KV:(kvh + 1) * D_KV] = o_gd.T

    def pair_scores(j):
        p, hp = divmod(j, N_KV_HEADS // 2)
        r0 = p * PAIR_ROWS
        blocks = []
        for kvh in (2 * hp, 2 * hp + 1):
            c0 = kvh * GQA_GROUP * HEAD_DIM
            q_t = jnp.concatenate(
                [q_s[c0 + g * HEAD_DIM:c0 + (g + 1) * HEAD_DIM, r0:r0 + PAIR_ROWS]
                 for g in range(GQA_GROUP)], axis=1)
            zeros = jnp.zeros_like(q_t)
            blocks.append(jnp.concatenate([q_t, zeros] if kvh % 2 == 0 else [zeros, q_t], axis=0))
        return jnp.dot(k_ext[r0:r0 + PAIR_KEYS, hp * LANES:(hp + 1) * LANES],
                       jnp.concatenate(blocks, axis=1), preferred_element_type=F32)

    def finish(i, s, after_row):
        p, kvh = divmod(i, N_KV_HEADS)
        r0 = p * PAIR_ROWS
        hc = slice(kvh * HEAD_DIM, (kvh + 1) * HEAD_DIM)
        store_block(p, kvh, _softmax_pv_t(
            s, vt_ext[hc, r0:r0 + PAIR_KEYS], bias_ref[kvh],
            _sink_row(sinks_ref, layer, kvh, PAIR_ROWS), after_row,
            mask_past=first if p == 0 else None))

    chunks = _act_chunks(h_s, w_in_ref, vgf_s, u_s, gb_s, ga_s)
    n_pairs = (TM // PAIR_ROWS) * N_KV_HEADS // 2
    width = GQA_GROUP * PAIR_ROWS
    chunks[0]()
    s_next = pair_scores(0)
    for j in range(n_pairs):
        s_pair = s_next
        chunks[2 * j + 1]()
        finish(2 * j, s_pair[:, :width], None)
        if j + 1 < n_pairs:
            s_next = pair_scores(j + 1)
            after_row = s_next[0:1, :width]
            chunks[2 * j + 2]()
        else:
            _out_gmlp_half(x_ref, w_out_ref, o_ref, y_s, 0, 2 * PROJ_COLS)
            after_row = o_ref[0:1, 0:width]
        finish(2 * j + 1, s_pair[:, width:], after_row)
        if j == 0:
            _layernorm_split(vgf_s, ln_g_ref, ln_b_ref, vlo_s, vhi_s)
        if j == 1:
            gmlp_mix()
        if j == 2:
            _gated_norm(u_s, gb_s, g_gmlp_ref, y_s, D_ATTN)
    _out_gmlp_half(x_ref, w_out_ref, o_ref, y_s, 2 * PROJ_COLS, D_MODEL)

    k_ext[0:WINDOW, :] = k_ext[TM:TM + WINDOW, :]
    vt_ext[:, 0:WINDOW] = vt_ext[:, TM:TM + WINDOW]

    _gated_norm(ao_s, ga_s, g_attn_ref, y_s, 0)
    _out_attn_half(w_out_ref, g_fin_ref, o_ref, y_s, final)


def _prompt_kernel(sinks_ref, x_ref, *refs, layer, cast_next):
    weights, refs = refs[:11], refs[11:]
    if cast_next:
        (w_in_f32, w_out_f32), refs = refs[:2], refs[2:]
    (o_ref, k_out_ref, v_out_ref), refs = refs[:3], refs[3:]
    if cast_next:
        (w_in_bf16, w_out_bf16), refs = refs[:2], refs[2:]
    scratch = refs
    k_ext, vt_ext = scratch[2], scratch[3]
    t = pl.program_id(1)

    @pl.when(t == 0)
    def _():
        k_ext[0:WINDOW, :] = jnp.zeros((WINDOW, D_KV), BF16)
        vt_ext[:, 0:WINDOW] = jnp.zeros((D_KV, WINDOW), BF16)

    for sub in range(TILES_PER_STEP):
        rows = slice(sub * TM, (sub + 1) * TM)
        last = sub == TILES_PER_STEP - 1
        _prompt_tile(sinks_ref, x_ref.at[rows, :], *weights, o_ref.at[rows, :],
                     (k_out_ref, v_out_ref) if last else None, *scratch,
                     layer=layer, first=(t == 0) if sub == 0 else None)
    if cast_next:
        w_in_bf16[...] = w_in_f32[...].astype(BF16)
        w_out_bf16[...] = w_out_f32[...].astype(BF16)


def _sample_kernel(sinks_ref, x_ref, ck_ref, cv_ref, w_in_ref, w_out_ref, wsp_ref, bsp_ref, bias_ref,
                   g_in_ref, g_attn_ref, g_gmlp_ref, ln_g_ref, ln_b_ref, g_fin_ref,
                   o_ref, nk_ref, nv_ref, vg_ref,
                   h_s, q_s, ga_s, gb_s, u_s, vlo_s, vhi_s, ao_s, y_s,
                   *, layer, t_len):
    final = layer == DEPTH - 1
    n_past = ck_ref.shape[1]
    q_cols = GQA_GROUP * t_len
    _rms_to_bf16(x_ref, g_in_ref, h_s)
    k, v = _proj_kvq(h_s, w_in_ref, q_s)

    nk_ref[:, 0:n_past - t_len, :] = ck_ref[:, t_len:, :]
    nv_ref[:, 0:n_past - t_len, :] = cv_ref[:, t_len:, :]
    nk_ref[:, n_past - t_len:, :] = k.reshape(NB, t_len, D_KV)
    nv_ref[:, n_past - t_len:, :] = v.reshape(NB, t_len, D_KV)

    kb = k.astype(BF16)
    vt_new = v.T.astype(BF16)
    vt_past = [cv_ref[b].T.astype(BF16) for b in range(NB)]

    def gmlp_mix():
        for j in range(N_PAIRS):
            w = _masked_spatial(wsp_ref, j, t_len)
            cols = slice(j * LANES, (j + 1) * LANES)
            rhs = jnp.concatenate(
                [jnp.concatenate([vlo_s[b * t_len:(b + 1) * t_len, cols],
                                  vhi_s[b * t_len:(b + 1) * t_len, cols]], axis=0)
                 for b in range(NB)], axis=1)
            mix = jnp.dot(w, rhs, preferred_element_type=F32)
            for b in range(NB):
                rws = slice(b * t_len, (b + 1) * t_len)
                u_s[rws, cols] = u_s[rws, cols] * (mix[:, b * LANES:(b + 1) * LANES] + bsp_ref[j])

    def scores(i):
        b, kvh = divmod(i, N_KV_HEADS)
        r0 = b * t_len
        c0 = kvh * GQA_GROUP * HEAD_DIM
        hc = slice(kvh * HEAD_DIM, (kvh + 1) * HEAD_DIM)
        q_rows = jnp.concatenate(
            [q_s[r0:r0 + t_len, c0 + g * HEAD_DIM:c0 + (g + 1) * HEAD_DIM]
             for g in range(GQA_GROUP)], axis=0)
        keys = jnp.concatenate([ck_ref[b, :, hc].astype(BF16), kb[r0:r0 + t_len, hc]], axis=0)
        return _scores_t(keys, q_rows)

    def finish(i, s, after_row):
        b, kvh = divmod(i, N_KV_HEADS)
        r0 = b * t_len
        c0 = kvh * GQA_GROUP * HEAD_DIM
        hc = slice(kvh * HEAD_DIM, (kvh + 1) * HEAD_DIM)
        vals_t = jnp.concatenate([vt_past[b][hc, :], vt_new[hc, r0:r0 + t_len]], axis=1)
        o = _softmax_pv_t(s, vals_t, bias_ref[kvh],
                          _sink_row(sinks_ref, layer, kvh, t_len), after_row).T
        for g in range(GQA_GROUP):
            cg = c0 + g * HEAD_DIM
            ao_s[r0:r0 + t_len, cg:cg + HEAD_DIM] = o[g * t_len:(g + 1) * t_len, :]

    chunks = _act_chunks(h_s, w_in_ref, vg_ref, u_s, gb_s, ga_s)
    per = (NB * N_KV_HEADS) // len(chunks)
    chunks[0]()
    s_next = [scores(i) for i in range(per)]
    for j in range(len(chunks)):
        s_cur = s_next
        if j + 1 < len(chunks):
            s_next = [scores((j + 1) * per + i) for i in range(per)]
            after = [s[0:1, :] for s in s_next]
            chunks[j + 1]()
        else:
            _out_gmlp_half(x_ref, w_out_ref, o_ref, y_s, 0, PROJ_COLS)
            after = [o_ref[0:1, 0:q_cols]] * per
        for i in range(per):
            finish(j * per + i, s_cur[i], after[i])
        if j == 1:
            _layernorm_split(vg_ref, ln_g_ref, ln_b_ref, vlo_s, vhi_s)
        if j == 3:
            gmlp_mix()
        if j == 5:
            _gated_norm(u_s, gb_s, g_gmlp_ref, y_s, D_ATTN)
    _out_gmlp_half(x_ref, w_out_ref, o_ref, y_s, PROJ_COLS, D_MODEL)

    _gated_norm(ao_s, ga_s, g_attn_ref, y_s, 0)
    _out_attn_half(w_out_ref, g_fin_ref, o_ref, y_s, final)


def _resident(shape):
    nd = len(shape)
    return pl.BlockSpec(shape, lambda *_: (0,) * nd, pipeline_mode=pl.Buffered(1))


def _layer_slice(arr, layer):
    tail = arr.shape[1:]
    return pl.BlockSpec((None,) + tail, lambda *_: (layer,) + (0,) * len(tail),
                        pipeline_mode=pl.Buffered(1))


def _weight_specs(layer, w_in, w_out, wsp, bsp, bias, vecs):
    return ([_layer_slice(w_in, 0), _layer_slice(w_out, 0), _layer_slice(wsp, layer),
             _layer_slice(bsp, layer), _resident(bias.shape)]
            + [_layer_slice(vec, layer) for vec in vecs[:-1]] + [_resident(vecs[-1].shape)])


def _prompt_layer(layer, x, sinks, w_in, w_out, wsp, bsp, bias, vecs, next_f32=None):
    batch, seq, _ = x.shape
    n_t = seq // (TILES_PER_STEP * TM)
    kv_shape = jax.ShapeDtypeStruct((batch, WINDOW, D_KV), F32)
    kv_spec = pl.BlockSpec((None, WINDOW, D_KV), lambda b, t: (b, 0, 0))
    x_spec = pl.BlockSpec((None, TILES_PER_STEP * TM, D_MODEL), lambda b, t: (b, t, 0))
    final = layer == DEPTH - 1
    in_specs = ([pl.BlockSpec(memory_space=pltpu.SMEM), x_spec]
                + _weight_specs(layer, w_in, w_out, wsp, bsp, bias, vecs))
    out_specs = [x_spec, kv_spec, kv_spec]
    out_shape = [jax.ShapeDtypeStruct(x.shape, F32), kv_shape, kv_shape]
    args = [sinks, x, w_in, w_out, wsp, bsp, bias, *vecs]
    if next_f32 is not None:
        rows = D_MODEL // (batch * n_t)
        assert rows * batch * n_t == D_MODEL and rows % BF16_ROWS == 0
        for w in next_f32:
            cols = w.shape[-1]
            in_specs.append(pl.BlockSpec((None, rows, cols),
                                         lambda b, t: (layer + 1, b * n_t + t, 0)))
            out_specs.append(pl.BlockSpec((None, rows, cols), lambda b, t: (0, b * n_t + t, 0)))
            out_shape.append(jax.ShapeDtypeStruct((1, D_MODEL, cols), BF16))
            args.append(w)
    return pl.pallas_call(
        functools.partial(_prompt_kernel, layer=layer, cast_next=next_f32 is not None),
        grid=(batch, n_t),
        in_specs=in_specs,
        out_specs=out_specs,
        out_shape=out_shape,
        scratch_shapes=[
            pltpu.VMEM((TM, D_MODEL), BF16),
            pltpu.VMEM((D_ATTN, TM), BF16),
            pltpu.VMEM((WINDOW + TM, D_KV), BF16),
            pltpu.VMEM((D_KV, WINDOW + TM), BF16),
            pltpu.VMEM((TM, D_ATTN), F32),
            pltpu.VMEM((TM, D_GMLP), F32),
            pltpu.VMEM((TM, D_GMLP), F32),
            pltpu.VMEM((TM, D_GMLP), F32),
            pltpu.VMEM((TM, D_GMLP), BF16),
            pltpu.VMEM((TM, D_GMLP), BF16),
            pltpu.VMEM((TM, D_ATTN), F32),
            pltpu.VMEM((TM, D_MODEL), BF16),
        ],
        compiler_params=pltpu.CompilerParams(
            dimension_semantics=("arbitrary", "arbitrary"), vmem_limit_bytes=VMEM_LIMIT_PROMPT),
        name="prompt_layer_final" if final else "prompt_layer",
    )(*args)


def _sample_layer(layer, x, ck, cv, sinks, w_in, w_out, wsp, bsp, bias, vecs, t_len):
    rows = x.shape[0]
    _, n_streams, n_past, _ = ck.shape
    step_rows = NB * t_len
    x_spec = pl.BlockSpec((step_rows, D_MODEL), lambda i: (i, 0))
    cin_spec = pl.BlockSpec((None, NB, n_past, D_KV), lambda i: (layer, i, 0, 0))
    cout_spec = pl.BlockSpec((NB, n_past, D_KV), lambda i: (i, 0, 0))
    vg_spec = pl.BlockSpec((step_rows, D_GMLP), lambda i: (i, 0))
    cache_shape = jax.ShapeDtypeStruct((n_streams, n_past, D_KV), F32)
    final = layer == DEPTH - 1
    return pl.pallas_call(
        functools.partial(_sample_kernel, layer=layer, t_len=t_len),
        grid=(n_streams // NB,),
        in_specs=[pl.BlockSpec(memory_space=pltpu.SMEM), x_spec, cin_spec, cin_spec]
        + _weight_specs(layer, w_in, w_out, wsp, bsp, bias, vecs),
        out_specs=[x_spec, cout_spec, cout_spec, vg_spec],
        out_shape=[jax.ShapeDtypeStruct(x.shape, F32), cache_shape, cache_shape,
                   jax.ShapeDtypeStruct((rows, D_GMLP), F32)],
        scratch_shapes=[
            pltpu.VMEM((step_rows, D_MODEL), BF16),
            pltpu.VMEM((step_rows, D_ATTN), BF16),
            pltpu.VMEM((step_rows, D_ATTN), F32),
            pltpu.VMEM((step_rows, D_GMLP), F32),
            pltpu.VMEM((step_rows, D_GMLP), F32),
            pltpu.VMEM((step_rows, D_GMLP), BF16),
            pltpu.VMEM((step_rows, D_GMLP), BF16),
            pltpu.VMEM((step_rows, D_ATTN), F32),
            pltpu.VMEM((step_rows, D_MODEL), BF16),
        ],
        compiler_params=pltpu.CompilerParams(
            dimension_semantics=("arbitrary",), vmem_limit_bytes=VMEM_LIMIT),
        name="sample_layer_final" if final else "sample_layer",
    )(sinks, x, ck, cv, w_in, w_out, wsp, bsp, bias, *vecs)


def kernel(x_prompt, x_sample, cache_k, cache_v, w_in, w_out, norm_in, rel_bias, sinks,
           norm_attn, norm_gmlp, ln_v_g, ln_v_b, w_spatial, b_spatial, norm_final):
    bp, seq, _ = x_prompt.shape
    bs, t_len, _ = x_sample.shape
    n_past = cache_k.shape[2]
    assert seq % (TILES_PER_STEP * TM) == 0 and TM % PAIR_ROWS == 0 and TM % GMLP_CHUNK == 0 and TM >= WINDOW
    assert bs % NB == 0 and t_len <= GMLP_CHUNK and n_past >= t_len
    assert TM % ROW_BLOCK == 0 and (NB * t_len) % ROW_BLOCK == 0
    assert (TM // PAIR_ROWS) * N_KV_HEADS == 4 * (D_GMLP // PROJ_COLS)

    bias_p = _prompt_bias_table(rel_bias)
    bias_s = _sample_bias_table(rel_bias, t_len, n_past)
    w_in_b = w_in[:1].astype(BF16)
    w_out_b = w_out[:1].astype(BF16)
    vecs = (norm_in[:, None], norm_attn[:, None], norm_gmlp[:, None],
            ln_v_g[:, None], ln_v_b[:, None], norm_final[None])
    wsp_p, bsp_p = _pair_spatial(w_spatial, b_spatial, GMLP_CHUNK)
    wsp_s, bsp_s = _pair_spatial(w_spatial, b_spatial, t_len)
    ck = cache_k.reshape(DEPTH, bs, n_past, D_KV)
    cv = cache_v.reshape(DEPTH, bs, n_past, D_KV)

    xp = x_prompt
    xs = x_sample.reshape(bs * t_len, D_MODEL)
    kp, vp, ks, vs, vgs = [], [], [], [], []
    for l in range(DEPTH):
        outs = _prompt_layer(l, xp, sinks, w_in_b, w_out_b, wsp_p, bsp_p, bias_p, vecs,
                             next_f32=(w_in, w_out) if l + 1 < DEPTH else None)
        xp, k_l, v_l = outs[:3]
        kp.append(k_l)
        vp.append(v_l)
        xs, nk, nv, vg = _sample_layer(l, xs, ck, cv, sinks, w_in_b, w_out_b, wsp_s, bsp_s,
                                       bias_s, vecs, t_len)
        if l + 1 < DEPTH:
            w_in_b, w_out_b = outs[3:]
        ks.append(nk)
        vs.append(nv)
        vgs.append(vg)

    def heads(rows, n):
        return jnp.stack(rows).reshape(DEPTH, n, -1, N_KV_HEADS, HEAD_DIM)

    return (xp, xs.reshape(bs, t_len, D_MODEL),
            heads(kp, bp), heads(vp, bp), heads(ks, bs), heads(vs, bs),
            jnp.stack(vgs).reshape(DEPTH, bs, t_len, N_GROUPS_B, HEAD_DIM))
```

```python
import functools
import math

import jax
import jax.numpy as jnp
from jax import lax
from jax.experimental import pallas as pl
from jax.experimental.pallas import tpu as pltpu

D_MODEL = 2048
DEPTH = 4
CHUNK = 64
HEAD_DIM = 64
D_ATTN = 1024
D_GMLP = 1024
N_HEADS = 16
N_KV_HEADS = 4
GQA_GROUP = 4
D_KV = 256
WINDOW = 128
N_BUCKETS = 32
MAX_DISTANCE = 128
GMLP_CHUNK = 128
N_GROUPS_B = 16
D_PROJ = 5632
EPS = 1e-6
NEG_INF = -1e30

O_Q, O_K, O_V, O_GA, O_U, O_VG, O_GB = 0, 1024, 1280, 1536, 2560, 3584, 4608

LANES = 128
BF16_ROWS = 16
N_PAIRS = N_GROUPS_B // 2
PAIR_ROWS = 2 * CHUNK
PAIR_KEYS = 4 * CHUNK
TM = 256
TILES_PER_STEP = 2
NB = 4
PROJ_COLS = 512
W_IN_COLS = D_PROJ + LANES
ROW_BLOCK = 32
VMEM_LIMIT = 58 * 1024 * 1024
VMEM_LIMIT_PROMPT = 61 * 1024 * 1024

F32 = jnp.float32
BF16 = jnp.bfloat16


def _rel_bucket(rel):
    nb = N_BUCKETS // 2
    max_exact = nb // 2
    base = jnp.where(rel > 0, nb, 0)
    n = jnp.abs(rel)
    nf = jnp.maximum(n, 1).astype(F32)
    large = max_exact + (jnp.log(nf / max_exact) / math.log(MAX_DISTANCE / max_exact)
                         * (nb - max_exact)).astype(jnp.int32)
    large = jnp.minimum(large, nb - 1)
    return base + jnp.where(n < max_exact, n, large)


def _band_bias(rel_bias, n_q, n_k, n_past):
    n = n_q + n_k - 1
    rel = jnp.arange(n) - (n_q - 1) - n_past
    v = rel_bias[_rel_bucket(rel)].astype(F32).T
    w = jnp.concatenate([v, jnp.zeros((N_HEADS, 1), F32)], axis=1)
    band = jnp.tile(w, (1, n_q))[:, :n_q * n].reshape(N_HEADS, n_q, n)
    return band[:, :, n_q - 1:]


def _prompt_bias_table(rel_bias):
    b = _band_bias(rel_bias, PAIR_ROWS, PAIR_KEYS, WINDOW)
    b = b.reshape(N_KV_HEADS, GQA_GROUP, PAIR_ROWS, PAIR_KEYS).transpose(0, 3, 1, 2)
    r = jnp.arange(PAIR_ROWS)[None, :]
    j = jnp.arange(PAIR_KEYS)[:, None]
    qc = r // CHUNK
    kc = j // CHUNK - 2
    in_window = (kc <= qc) & (kc >= qc - 2)
    t = jnp.where(in_window[None, :, None, :], b, NEG_INF)
    return t.reshape(N_KV_HEADS, PAIR_KEYS, GQA_GROUP * PAIR_ROWS)


def _sample_bias_table(rel_bias, n_q, n_past):
    b = _band_bias(rel_bias, n_q, n_past + n_q, n_past)
    return jnp.swapaxes(b.reshape(N_KV_HEADS, GQA_GROUP * n_q, n_past + n_q), 1, 2)


def _pair_spatial(w_s, b_s, rows):
    depth = w_s.shape[0]
    w = w_s[:, :, :rows, :rows].reshape(depth, N_PAIRS, 2, rows, rows)
    w = jnp.transpose(w, (0, 1, 3, 2, 4)).reshape(depth, N_PAIRS, rows, 2 * rows)
    b = b_s[:, :, :rows].reshape(depth, N_PAIRS, 2, rows)
    b = jnp.repeat(jnp.transpose(b, (0, 1, 3, 2)), LANES // 2, axis=3)
    return w, b


def _silu(x):
    return x / (1.0 + jnp.exp(-x))


def _gelu(x):
    return 0.5 * x * (1.0 + lax.erf(x * math.sqrt(0.5)))


def _rms_to_bf16(x_ref, g_ref, h_s):
    for r in range(0, x_ref.shape[0], ROW_BLOCK):
        x = x_ref[r:r + ROW_BLOCK, :]
        inv = lax.rsqrt(jnp.mean(x * x, axis=-1, keepdims=True) + EPS)
        h_s[r:r + ROW_BLOCK, :] = (x * inv * g_ref[...]).astype(BF16)


def _proj(h_s, w_in_ref, off, width):
    return jnp.dot(h_s[...], w_in_ref[:, off:off + width], preferred_element_type=F32)


SCORE_SCALE = HEAD_DIM ** -0.5


def _proj_kvq(h_s, w_in_ref, q_s, q_transposed=False):
    kv = _proj(h_s, w_in_ref, O_K, 2 * D_KV)
    for c in range(0, D_ATTN, PROJ_COLS):
        q = _proj(h_s, w_in_ref, O_Q + c, PROJ_COLS)
        if q_transposed:
            q_s[c:c + PROJ_COLS, :] = q.T.astype(BF16)
        else:
            q_s[:, c:c + PROJ_COLS] = (q * SCORE_SCALE).astype(BF16)
    return kv[:, :D_KV], kv[:, D_KV:]


def _act_chunks(h_s, w_in_ref, vgf_s, u_s, gb_s, ga_s):
    def make(dst, off, act, c):
        def run():
            dst[:, c:c + PROJ_COLS] = act(_proj(h_s, w_in_ref, off + c, PROJ_COLS))
        return run
    runs = [make(dst, off, act, c)
            for dst, off, act in ((ga_s, O_GA, _silu), (vgf_s, O_VG, _gelu), (u_s, O_U, _gelu),
                                  (gb_s, O_GB, _silu))
            for c in range(0, D_GMLP, PROJ_COLS)]
    n = D_GMLP // PROJ_COLS
    return runs[:1] + runs[n:] + runs[1:n]


def _layernorm_split(vgf_s, ln_g_ref, ln_b_ref, vlo_s, vhi_s):
    low = (lax.broadcasted_iota(jnp.int32, (ROW_BLOCK, D_GMLP), 1) % LANES) < (LANES // 2)
    for r in range(0, vgf_s.shape[0], ROW_BLOCK):
        g = vgf_s[r:r + ROW_BLOCK, :]
        gc = g - jnp.mean(g, axis=-1, keepdims=True)
        y = gc * lax.rsqrt(jnp.mean(gc * gc, axis=-1, keepdims=True) + EPS)
        y = y * ln_g_ref[...] + ln_b_ref[...]
        vgf_s[r:r + ROW_BLOCK, :] = y
        vlo_s[r:r + ROW_BLOCK, :] = jnp.where(low, y, 0.0).astype(BF16)
        vhi_s[r:r + ROW_BLOCK, :] = jnp.where(low, 0.0, y).astype(BF16)


def _scores_t(keys, q_rows):
    return lax.dot_general(keys, q_rows, (((1,), (1,)), ((), ())), preferred_element_type=F32)


def _zero_after(row):
    u = lax.bitcast_convert_type(row, jnp.uint32)
    u = lax.shift_right_logical(lax.shift_right_logical(u, jnp.uint32(16)), jnp.uint32(16))
    return lax.bitcast_convert_type(u, F32)


def _softmax_pv_t(s, vals_t, bias_t, sink_row, after_row, mask_past=None):
    s = s + bias_t
    if mask_past is not None:
        s = jnp.concatenate([jnp.where(mask_past, NEG_INF, s[:WINDOW]), s[WINDOW:]], axis=0)
    m = jnp.maximum(jnp.max(s, axis=0, keepdims=True), sink_row)
    e = jnp.exp(s - m)
    l = jnp.sum(e, axis=0, keepdims=True) + jnp.exp(sink_row - m)
    if after_row is None:
        p = e.astype(BF16)
    else:
        tail = e.shape[0] - BF16_ROWS
        p = jnp.concatenate([e[:tail].astype(BF16),
                             (e[tail:] + _zero_after(after_row)).astype(BF16)], axis=0)
    o = jnp.dot(vals_t, p, preferred_element_type=F32)
    return o * (1.0 / l)


def _sink_row(sinks_ref, layer, kvh, n):
    shape = (1, GQA_GROUP * n)
    g = lax.broadcasted_iota(jnp.int32, shape, 1) // n
    out = jnp.full(shape, sinks_ref[layer, kvh * GQA_GROUP], F32)
    for i in range(1, GQA_GROUP):
        out = jnp.where(g == i, sinks_ref[layer, kvh * GQA_GROUP + i], out)
    return out


def _masked_spatial(wsp_ref, j, rows):
    w = wsp_ref[j]
    i = lax.broadcasted_iota(jnp.int32, w.shape, 0)
    c = lax.broadcasted_iota(jnp.int32, w.shape, 1) % rows
    return jnp.where(c <= i, w, 0.0).astype(BF16)


def _gated_norm(src_s, gate_s, g_ref, y_s, col0):
    rows, width = src_s.shape
    for r in range(0, rows, ROW_BLOCK):
        a = src_s[r:r + ROW_BLOCK, :]
        inv = lax.rsqrt(jnp.mean(a * a, axis=-1, keepdims=True) + EPS)
        y_s[r:r + ROW_BLOCK, col0:col0 + width] = (
            a * inv * g_ref[...] * gate_s[r:r + ROW_BLOCK, :]).astype(BF16)


def _out_gmlp_half(x_ref, w_out_ref, o_ref, y_s, col_lo=0, col_hi=D_MODEL):
    for c in range(col_lo, col_hi, PROJ_COLS):
        o_ref[:, c:c + PROJ_COLS] = x_ref[:, c:c + PROJ_COLS] + jnp.dot(
            y_s[:, D_ATTN:], w_out_ref[D_ATTN:, c:c + PROJ_COLS], preferred_element_type=F32)


def _out_attn_half(w_out_ref, g_fin_ref, o_ref, y_s, final):
    rows = o_ref.shape[0]
    for c in range(0, D_MODEL, PROJ_COLS):
        o_ref[:, c:c + PROJ_COLS] = o_ref[:, c:c + PROJ_COLS] + jnp.dot(
            y_s[:, :D_ATTN], w_out_ref[:D_ATTN, c:c + PROJ_COLS], preferred_element_type=F32)
    if final:
        for r in range(0, rows, ROW_BLOCK):
            xn = o_ref[r:r + ROW_BLOCK, :]
            inv = lax.rsqrt(jnp.mean(xn * xn, axis=-1, keepdims=True) + EPS)
            o_ref[r:r + ROW_BLOCK, :] = xn * inv * g_fin_ref[...]


def _prompt_tile(sinks_ref, x_ref, w_in_ref, w_out_ref, wsp_ref, bsp_ref, bias_ref,
                 g_in_ref, g_attn_ref, g_gmlp_ref, ln_g_ref, ln_b_ref, g_fin_ref,
                 o_ref, kv_out_refs,
                 h_s, q_s, k_ext, vt_ext, ga_s, gb_s, u_s, vgf_s, vlo_s, vhi_s, ao_s, y_s,
                 *, layer, first):
    final = layer == DEPTH - 1
    _rms_to_bf16(x_ref, g_in_ref, h_s)
    k, v = _proj_kvq(h_s, w_in_ref, q_s, q_transposed=True)
    k_ext[WINDOW:, :] = (k * SCORE_SCALE).astype(BF16)
    vt_ext[:, WINDOW:] = v.T.astype(BF16)
    if kv_out_refs is not None:
        kv_out_refs[0][...] = k[TM - WINDOW:, :]
        kv_out_refs[1][...] = v[TM - WINDOW:, :]

    def gmlp_mix():
        for j in range(N_PAIRS):
            w = _masked_spatial(wsp_ref, j, GMLP_CHUNK)
            cols = slice(j * LANES, (j + 1) * LANES)
            for c in range(TM // GMLP_CHUNK):
                rws = slice(c * GMLP_CHUNK, (c + 1) * GMLP_CHUNK)
                rhs = jnp.concatenate([vlo_s[rws, cols], vhi_s[rws, cols]], axis=0)
                mix = jnp.dot(w, rhs, preferred_element_type=F32) + bsp_ref[j]
                u_s[rws, cols] = u_s[rws, cols] * mix

    def store_block(p, kvh, o_t):
        o_gd = jnp.concatenate(
            [o_t[:, g * PAIR_ROWS:(g + 1) * PAIR_ROWS] for g in range(GQA_GROUP)], axis=0)
        ao_s[p * PAIR_ROWS:(p + 1) * PAIR_ROWS, kvh * D_KV:(kvh + 1) * D_KV] = o_gd.T

    def pair_scores(j):
        p, hp = divmod(j, N_KV_HEADS // 2)
        r0 = p * PAIR_ROWS
        blocks = []
        for kvh in (2 * hp, 2 * hp + 1):
            c0 = kvh * GQA_GROUP * HEAD_DIM
            q_t = jnp.concatenate(
                [q_s[c0 + g * HEAD_DIM:c0 + (g + 1) * HEAD_DIM, r0:r0 + PAIR_ROWS]
                 for g in range(GQA_GROUP)], axis=1)
            zeros = jnp.zeros_like(q_t)
            blocks.append(jnp.concatenate([q_t, zeros] if kvh % 2 == 0 else [zeros, q_t], axis=0))
        return jnp.dot(k_ext[r0:r0 + PAIR_KEYS, hp * LANES:(hp + 1) * LANES],
                       jnp.concatenate(blocks, axis=1), preferred_element_type=F32)

    def finish(i, s, after_row):
        p, kvh = divmod(i, N_KV_HEADS)
        r0 = p * PAIR_ROWS
        hc = slice(kvh * HEAD_DIM, (kvh + 1) * HEAD_DIM)
        store_block(p, kvh, _softmax_pv_t(
            s, vt_ext[hc, r0:r0 + PAIR_KEYS], bias_ref[kvh],
            _sink_row(sinks_ref, layer, kvh, PAIR_ROWS), after_row,
            mask_past=first if p == 0 else None))

    chunks = _act_chunks(h_s, w_in_ref, vgf_s, u_s, gb_s, ga_s)
    n_pairs = (TM // PAIR_ROWS) * N_KV_HEADS // 2
    width = GQA_GROUP * PAIR_ROWS
    chunks[0]()
    s_next = pair_scores(0)
    for j in range(n_pairs):
        s_pair = s_next
        chunks[2 * j + 1]()
        finish(2 * j, s_pair[:, :width], None)
        if j + 1 < n_pairs:
            s_next = pair_scores(j + 1)
            after_row = s_next[0:1, :width]
            chunks[2 * j + 2]()
        else:
            _out_gmlp_half(x_ref, w_out_ref, o_ref, y_s, 0, 2 * PROJ_COLS)
            after_row = o_ref[0:1, 0:width]
        finish(2 * j + 1, s_pair[:, width:], after_row)
        if j == 0:
            _layernorm_split(vgf_s, ln_g_ref, ln_b_ref, vlo_s, vhi_s)
        if j == 1:
            gmlp_mix()
        if j == 2:
            _gated_norm(u_s, gb_s, g_gmlp_ref, y_s, D_ATTN)
    _out_gmlp_half(x_ref, w_out_ref, o_ref, y_s, 2 * PROJ_COLS, D_MODEL)

    k_ext[0:WINDOW, :] = k_ext[TM:TM + WINDOW, :]
    vt_ext[:, 0:WINDOW] = vt_ext[:, TM:TM + WINDOW]

    _gated_norm(ao_s, ga_s, g_attn_ref, y_s, 0)
    _out_attn_half(w_out_ref, g_fin_ref, o_ref, y_s, final)


def _prompt_kernel(sinks_ref, x_ref, *refs, layer, cast_next):
    weights, refs = refs[:11], refs[11:]
    if cast_next:
        (w_in_f32, w_out_f32), refs = refs[:2], refs[2:]
    (o_ref, k_out_ref, v_out_ref), refs = refs[:3], refs[3:]
    if cast_next:
        (w_in_bf16, w_out_bf16), refs = refs[:2], refs[2:]
    scratch = refs
    k_ext, vt_ext = scratch[2], scratch[3]
    t = pl.program_id(1)

    @pl.when(t == 0)
    def _():
        k_ext[0:WINDOW, :] = jnp.zeros((WINDOW, D_KV), BF16)
        vt_ext[:, 0:WINDOW] = jnp.zeros((D_KV, WINDOW), BF16)

    for sub in range(TILES_PER_STEP):
        rows = slice(sub * TM, (sub + 1) * TM)
        last = sub == TILES_PER_STEP - 1
        _prompt_tile(sinks_ref, x_ref.at[rows, :], *weights, o_ref.at[rows, :],
                     (k_out_ref, v_out_ref) if last else None, *scratch,
                     layer=layer, first=(t == 0) if sub == 0 else None)
    if cast_next:
        w_in_bf16[:, :D_PROJ] = w_in_f32[...].astype(BF16)
        w_in_bf16[:, D_PROJ:] = jnp.zeros((w_in_bf16.shape[0], W_IN_COLS - D_PROJ), BF16)
        w_out_bf16[...] = w_out_f32[...].astype(BF16)


def _sample_kernel(sinks_ref, x_ref, ck_ref, cv_ref, w_in_ref, w_out_ref, wsp_ref, bsp_ref, bias_ref,
                   g_in_ref, g_attn_ref, g_gmlp_ref, ln_g_ref, ln_b_ref, g_fin_ref,
                   o_ref, nk_ref, nv_ref, vg_ref,
                   h_s, q_s, ga_s, gb_s, u_s, vlo_s, vhi_s, ao_s, y_s,
                   *, layer, t_len):
    final = layer == DEPTH - 1
    n_past = ck_ref.shape[1]
    q_cols = GQA_GROUP * t_len
    _rms_to_bf16(x_ref, g_in_ref, h_s)
    k, v = _proj_kvq(h_s, w_in_ref, q_s)

    nk_ref[:, 0:n_past - t_len, :] = ck_ref[:, t_len:, :]
    nv_ref[:, 0:n_past - t_len, :] = cv_ref[:, t_len:, :]
    nk_ref[:, n_past - t_len:, :] = k.reshape(NB, t_len, D_KV)
    nv_ref[:, n_past - t_len:, :] = v.reshape(NB, t_len, D_KV)

    kb = k.astype(BF16)
    vt_new = v.T.astype(BF16)
    vt_past = [cv_ref[b].T.astype(BF16) for b in range(NB)]

    def gmlp_mix():
        for j in range(N_PAIRS):
            w = _masked_spatial(wsp_ref, j, t_len)
            cols = slice(j * LANES, (j + 1) * LANES)
            rhs = jnp.concatenate(
                [jnp.concatenate([vlo_s[b * t_len:(b + 1) * t_len, cols],
                                  vhi_s[b * t_len:(b + 1) * t_len, cols]], axis=0)
                 for b in range(NB)], axis=1)
            mix = jnp.dot(w, rhs, preferred_element_type=F32)
            for b in range(NB):
                rws = slice(b * t_len, (b + 1) * t_len)
                u_s[rws, cols] = u_s[rws, cols] * (mix[:, b * LANES:(b + 1) * LANES] + bsp_ref[j])

    def scores(i):
        b, kvh = divmod(i, N_KV_HEADS)
        r0 = b * t_len
        c0 = kvh * GQA_GROUP * HEAD_DIM
        hc = slice(kvh * HEAD_DIM, (kvh + 1) * HEAD_DIM)
        q_rows = jnp.concatenate(
            [q_s[r0:r0 + t_len, c0 + g * HEAD_DIM:c0 + (g + 1) * HEAD_DIM]
             for g in range(GQA_GROUP)], axis=0)
        keys = jnp.concatenate([ck_ref[b, :, hc].astype(BF16), kb[r0:r0 + t_len, hc]], axis=0)
        return _scores_t(keys, q_rows)

    def finish(i, s, after_row):
        b, kvh = divmod(i, N_KV_HEADS)
        r0 = b * t_len
        c0 = kvh * GQA_GROUP * HEAD_DIM
        hc = slice(kvh * HEAD_DIM, (kvh + 1) * HEAD_DIM)
        vals_t = jnp.concatenate([vt_past[b][hc, :], vt_new[hc, r0:r0 + t_len]], axis=1)
        o = _softmax_pv_t(s, vals_t, bias_ref[kvh],
                          _sink_row(sinks_ref, layer, kvh, t_len), after_row).T
        for g in range(GQA_GROUP):
            cg = c0 + g * HEAD_DIM
            ao_s[r0:r0 + t_len, cg:cg + HEAD_DIM] = o[g * t_len:(g + 1) * t_len, :]

    chunks = _act_chunks(h_s, w_in_ref, vg_ref, u_s, gb_s, ga_s)
    per = (NB * N_KV_HEADS) // len(chunks)
    chunks[0]()
    s_next = [scores(i) for i in range(per)]
    for j in range(len(chunks)):
        s_cur = s_next
        if j + 1 < len(chunks):
            s_next = [scores((j + 1) * per + i) for i in range(per)]
            after = [s[0:1, :] for s in s_next]
            chunks[j + 1]()
        else:
            _out_gmlp_half(x_ref, w_out_ref, o_ref, y_s, 0, PROJ_COLS)
            after = [o_ref[0:1, 0:q_cols]] * per
        for i in range(per):
            finish(j * per + i, s_cur[i], after[i])
        if j == 1:
            _layernorm_split(vg_ref, ln_g_ref, ln_b_ref, vlo_s, vhi_s)
        if j == 3:
            gmlp_mix()
        if j == 5:
            _gated_norm(u_s, gb_s, g_gmlp_ref, y_s, D_ATTN)
    _out_gmlp_half(x_ref, w_out_ref, o_ref, y_s, PROJ_COLS, D_MODEL)

    _gated_norm(ao_s, ga_s, g_attn_ref, y_s, 0)
    _out_attn_half(w_out_ref, g_fin_ref, o_ref, y_s, final)


def _resident(shape):
    nd = len(shape)
    return pl.BlockSpec(shape, lambda *_: (0,) * nd, pipeline_mode=pl.Buffered(1))


def _layer_slice(arr, layer):
    tail = arr.shape[1:]
    return pl.BlockSpec((None,) + tail, lambda *_: (layer,) + (0,) * len(tail),
                        pipeline_mode=pl.Buffered(1))


def _weight_specs(layer, w_in, w_out, wsp, bsp, bias, vecs):
    return ([_layer_slice(w_in, 0), _layer_slice(w_out, 0), _layer_slice(wsp, layer),
             _layer_slice(bsp, layer), _resident(bias.shape)]
            + [_layer_slice(vec, layer) for vec in vecs[:-1]] + [_resident(vecs[-1].shape)])


def _prompt_layer(layer, x, sinks, w_in, w_out, wsp, bsp, bias, vecs, next_f32=None):
    batch, seq, _ = x.shape
    n_t = seq // (TILES_PER_STEP * TM)
    kv_shape = jax.ShapeDtypeStruct((batch, WINDOW, D_KV), F32)
    kv_spec = pl.BlockSpec((None, WINDOW, D_KV), lambda b, t: (b, 0, 0))
    x_spec = pl.BlockSpec((None, TILES_PER_STEP * TM, D_MODEL), lambda b, t: (b, t, 0))
    final = layer == DEPTH - 1
    in_specs = ([pl.BlockSpec(memory_space=pltpu.SMEM), x_spec]
                + _weight_specs(layer, w_in, w_out, wsp, bsp, bias, vecs))
    out_specs = [x_spec, kv_spec, kv_spec]
    out_shape = [jax.ShapeDtypeStruct(x.shape, F32), kv_shape, kv_shape]
    args = [sinks, x, w_in, w_out, wsp, bsp, bias, *vecs]
    if next_f32 is not None:
        rows = D_MODEL // (batch * n_t)
        assert rows * batch * n_t == D_MODEL and rows % BF16_ROWS == 0
        for w, out_cols in zip(next_f32, (W_IN_COLS, D_MODEL)):
            cols = w.shape[-1]
            in_specs.append(pl.BlockSpec((None, rows, cols),
                                         lambda b, t: (layer + 1, b * n_t + t, 0)))
            out_specs.append(pl.BlockSpec((None, rows, out_cols),
                                          lambda b, t: (0, b * n_t + t, 0)))
            out_shape.append(jax.ShapeDtypeStruct((1, D_MODEL, out_cols), BF16))
            args.append(w)
    return pl.pallas_call(
        functools.partial(_prompt_kernel, layer=layer, cast_next=next_f32 is not None),
        grid=(batch, n_t),
        in_specs=in_specs,
        out_specs=out_specs,
        out_shape=out_shape,
        scratch_shapes=[
            pltpu.VMEM((TM, D_MODEL), BF16),
            pltpu.VMEM((D_ATTN, TM), BF16),
            pltpu.VMEM((WINDOW + TM, D_KV), BF16),
            pltpu.VMEM((D_KV, WINDOW + TM), BF16),
            pltpu.VMEM((TM, D_ATTN), F32),
            pltpu.VMEM((TM, D_GMLP), F32),
            pltpu.VMEM((TM, D_GMLP), F32),
            pltpu.VMEM((TM, D_GMLP), F32),
            pltpu.VMEM((TM, D_GMLP), BF16),
            pltpu.VMEM((TM, D_GMLP), BF16),
            pltpu.VMEM((TM, D_ATTN), F32),
            pltpu.VMEM((TM, D_MODEL), BF16),
        ],
        compiler_params=pltpu.CompilerParams(
            dimension_semantics=("arbitrary", "arbitrary"), vmem_limit_bytes=VMEM_LIMIT_PROMPT),
        name="prompt_layer_final" if final else "prompt_layer",
    )(*args)


def _sample_layer(layer, x, ck, cv, sinks, w_in, w_out, wsp, bsp, bias, vecs, t_len):
    rows = x.shape[0]
    _, n_streams, n_past, _ = ck.shape
    step_rows = NB * t_len
    x_spec = pl.BlockSpec((step_rows, D_MODEL), lambda i: (i, 0))
    cin_spec = pl.BlockSpec((None, NB, n_past, D_KV), lambda i: (layer, i, 0, 0))
    cout_spec = pl.BlockSpec((NB, n_past, D_KV), lambda i: (i, 0, 0))
    vg_spec = pl.BlockSpec((step_rows, D_GMLP), lambda i: (i, 0))
    cache_shape = jax.ShapeDtypeStruct((n_streams, n_past, D_KV), F32)
    final = layer == DEPTH - 1
    return pl.pallas_call(
        functools.partial(_sample_kernel, layer=layer, t_len=t_len),
        grid=(n_streams // NB,),
        in_specs=[pl.BlockSpec(memory_space=pltpu.SMEM), x_spec, cin_spec, cin_spec]
        + _weight_specs(layer, w_in, w_out, wsp, bsp, bias, vecs),
        out_specs=[x_spec, cout_spec, cout_spec, vg_spec],
        out_shape=[jax.ShapeDtypeStruct(x.shape, F32), cache_shape, cache_shape,
                   jax.ShapeDtypeStruct((rows, D_GMLP), F32)],
        scratch_shapes=[
            pltpu.VMEM((step_rows, D_MODEL), BF16),
            pltpu.VMEM((step_rows, D_ATTN), BF16),
            pltpu.VMEM((step_rows, D_ATTN), F32),
            pltpu.VMEM((step_rows, D_GMLP), F32),
            pltpu.VMEM((step_rows, D_GMLP), F32),
            pltpu.VMEM((step_rows, D_GMLP), BF16),
            pltpu.VMEM((step_rows, D_GMLP), BF16),
            pltpu.VMEM((step_rows, D_ATTN), F32),
            pltpu.VMEM((step_rows, D_MODEL), BF16),
        ],
        compiler_params=pltpu.CompilerParams(
            dimension_semantics=("arbitrary",), vmem_limit_bytes=VMEM_LIMIT),
        name="sample_layer_final" if final else "sample_layer",
    )(sinks, x, ck, cv, w_in, w_out, wsp, bsp, bias, *vecs)


def kernel(x_prompt, x_sample, cache_k, cache_v, w_in, w_out, norm_in, rel_bias, sinks,
           norm_attn, norm_gmlp, ln_v_g, ln_v_b, w_spatial, b_spatial, norm_final):
    bp, seq, _ = x_prompt.shape
    bs, t_len, _ = x_sample.shape
    n_past = cache_k.shape[2]
    assert seq % (TILES_PER_STEP * TM) == 0 and TM % PAIR_ROWS == 0 and TM % GMLP_CHUNK == 0 and TM >= WINDOW
    assert bs % NB == 0 and t_len <= GMLP_CHUNK and n_past >= t_len
    assert TM % ROW_BLOCK == 0 and (NB * t_len) % ROW_BLOCK == 0
    assert (TM // PAIR_ROWS) * N_KV_HEADS == 4 * (D_GMLP // PROJ_COLS)

    bias_p = _prompt_bias_table(rel_bias)
    bias_s = _sample_bias_table(rel_bias, t_len, n_past)
    w_in_b = jnp.pad(w_in[:1].astype(BF16), ((0, 0), (0, 0), (0, W_IN_COLS - D_PROJ)))
    w_out_b = w_out[:1].astype(BF16)
    vecs = (norm_in[:, None], norm_attn[:, None], norm_gmlp[:, None],
            ln_v_g[:, None], ln_v_b[:, None], norm_final[None])
    wsp_p, bsp_p = _pair_spatial(w_spatial, b_spatial, GMLP_CHUNK)
    wsp_s, bsp_s = _pair_spatial(w_spatial, b_spatial, t_len)
    ck = cache_k.reshape(DEPTH, bs, n_past, D_KV)
    cv = cache_v.reshape(DEPTH, bs, n_past, D_KV)

    xp = x_prompt
    xs = x_sample.reshape(bs * t_len, D_MODEL)
    kp, vp, ks, vs, vgs = [], [], [], [], []
    for l in range(DEPTH):
        outs = _prompt_layer(l, xp, sinks, w_in_b, w_out_b, wsp_p, bsp_p, bias_p, vecs,
                             next_f32=(w_in, w_out) if l + 1 < DEPTH else None)
        xp, k_l, v_l = outs[:3]
        kp.append(k_l)
        vp.append(v_l)
        xs, nk, nv, vg = _sample_layer(l, xs, ck, cv, sinks, w_in_b, w_out_b, wsp_s, bsp_s,
                                       bias_s, vecs, t_len)
        if l + 1 < DEPTH:
            w_in_b, w_out_b = outs[3:]
        ks.append(nk)
        vs.append(nv)
        vgs.append(vg)

    def heads(rows, n):
        return jnp.stack(rows).reshape(DEPTH, n, -1, N_KV_HEADS, HEAD_DIM)

    return (xp, xs.reshape(bs, t_len, D_MODEL),
            heads(kp, bp), heads(vp, bp), heads(ks, bs), heads(vs, bs),
            jnp.stack(vgs).reshape(DEPTH, bs, t_len, N_GROUPS_B, HEAD_DIM))
```
